```python
import jax, jax.numpy as jnp
from jax import lax
import numpy as np

D_MODEL = 1024
BATCH = 2
SEQ = 8192
DEPTH = 4
DEC_BATCH = 128
DEC_SEQ = 4
PAST_LEN = 8192
PAGE_SIZE = 128

N_MIXERS = 4
N_LAYERS_A = (DEPTH + 3) // N_MIXERS
N_LAYERS_B = (DEPTH + 2) // N_MIXERS
N_LAYERS_C = (DEPTH + 1) // N_MIXERS
N_LAYERS_D = DEPTH // N_MIXERS
D_FF = 4 * D_MODEL
EPS = 1e-6
QBLOCK = 128
CHUNK = 128
D_A = D_MODEL
G_A = 8
C_A = D_A // G_A
POOL_WINDOWS = (2, 4, 8, 16)
G_B = len(POOL_WINDOWS)
C_B = D_MODEL // G_B
POOL_HIST = max(POOL_WINDOWS) - 1
H_C = 16
DN_C = 64
DR_C = 32
DV_C = 64
Q_RANK = 384
KV_RANK = 256
ROPE_THETA = 10000.0
MLA_SCALE = (DN_C + DR_C) ** -0.5
H_D = 16
DH_D = 64
FOX_SCALE = DH_D ** -0.5
FORGET_BIAS = 3.0

kernel_name = 'hybrid_gmlp_pool_mla_fox_step'


def rms_norm(x, g):
    xf = x.astype(jnp.float32)
    y = xf * lax.rsqrt(jnp.mean(xf * xf, axis=-1, keepdims=True) + EPS)
    return (y * g.astype(jnp.float32)).astype(x.dtype)


def layer_norm(x, g, b):
    xf = x.astype(jnp.float32)
    mu = jnp.mean(xf, axis=-1, keepdims=True)
    var = jnp.mean(jnp.square(xf - mu), axis=-1, keepdims=True)
    y = (xf - mu) * lax.rsqrt(var + EPS) * g.astype(jnp.float32) + b.astype(jnp.float32)
    return y.astype(x.dtype)


def sq_relu_mlp(h, w_up, w_down):
    return jnp.square(jax.nn.relu(h @ w_up)) @ w_down


def causal_mask(s, q_pos, k_pos):
    return jnp.where(k_pos[None, :] <= q_pos[:, None], s, -jnp.inf)


def softmax_segments(scores):
    p = jax.nn.softmax(jnp.concatenate(scores, axis=-1), axis=-1)
    cuts = np.cumsum([a.shape[-1] for a in scores])[:-1].tolist()
    return jnp.split(p, cuts, axis=-1)


def sweep_queries(attend, q_parts, q_pos):
    T = q_pos.shape[0]
    nb = T // QBLOCK
    def blocks(a):
        return jnp.moveaxis(a.reshape(a.shape[0], nb, QBLOCK, *a.shape[2:]), 1, 0)
    out = lax.map(lambda a: attend(*a), (*[blocks(a) for a in q_parts], q_pos.reshape(nb, QBLOCK)))
    return jnp.moveaxis(out, 0, 1).reshape(q_parts[0].shape[0], T, out.shape[-1])


def gather_pages(pool, layer, page_table):
    g = pool[layer, page_table]
    return g.reshape(g.shape[0], g.shape[1] * g.shape[2], *g.shape[3:])


def rope_tables(pos):
    inv = ROPE_THETA ** (-jnp.arange(0, DR_C, 2, dtype=jnp.float32) / DR_C)
    ang = pos.astype(jnp.float32)[:, None] * inv[None, :]
    return jnp.cos(ang), jnp.sin(ang)


def apply_rope(x, cos, sin):
    xf = x.astype(jnp.float32)
    x1, x2 = xf[..., : DR_C // 2], xf[..., DR_C // 2:]
    return jnp.concatenate([x1 * cos - x2 * sin, x1 * sin + x2 * cos], axis=-1).astype(x.dtype)


def chunk_mlp(h, w_in, g_v, b_v, w_s, b_s, w_out):
    B, T, _ = h.shape
    u, v = jnp.split(jax.nn.gelu(h @ w_in), 2, axis=-1)
    v = layer_norm(v, g_v, b_v)
    L = min(T, CHUNK)
    n = T // L
    ws = jnp.where(jnp.tril(jnp.ones((L, L), bool))[None], w_s[:, :L, :L], 0.0)
    vc = v.reshape(B, n, L, G_A, C_A)
    mixed = jnp.einsum('gts,bnsgc->bntgc', ws, vc) + b_s[:, :L].T[None, None, :, :, None]
    y = u * mixed.reshape(B, T, D_A)
    return y @ w_out, v


def pool_mix(h, hist, pos, w_in, w_grp, scale):
    z = h @ w_in
    ze = jnp.concatenate([hist.astype(z.dtype), z], axis=1)
    zf = ze.astype(jnp.float32)
    cs = jnp.concatenate([jnp.zeros_like(zf[:, :1]), jnp.cumsum(zf, axis=1)], axis=1)
    T = z.shape[1]
    end = cs[:, POOL_HIST + 1:POOL_HIST + 1 + T]
    outs = []
    for g, w in enumerate(POOL_WINDOWS):
        sl = slice(g * C_B, (g + 1) * C_B)
        start = cs[:, POOL_HIST + 1 - w:POOL_HIST + 1 - w + T, sl]
        cnt = jnp.minimum(w, pos + 1).astype(jnp.float32)[None, :, None]
        outs.append((end[..., sl] - start) / cnt)
    d = (jnp.concatenate(outs, axis=-1) - zf[:, POOL_HIST:]).astype(h.dtype)
    B = h.shape[0]
    y = jnp.einsum('btgc,gce->btge', d.reshape(B, T, G_B, C_B), w_grp).reshape(B, T, D_MODEL) * scale
    return y, ze[:, -POOL_HIST:]


def mla_project(h, pos, w_in, g_q, g_kv, w_uq):
    B, T, _ = h.shape
    c_q, c_kv, k_r = jnp.split(h @ w_in, [Q_RANK, Q_RANK + KV_RANK], axis=-1)
    q = (rms_norm(c_q, g_q) @ w_uq).reshape(B, T, H_C, DN_C + DR_C)
    cos, sin = rope_tables(pos)
    q_r = apply_rope(q[..., DN_C:], cos[:, None, :], sin[:, None, :])
    return q[..., :DN_C], q_r, rms_norm(c_kv, g_kv), apply_rope(k_r, cos, sin)


def mla_attend(q_n, q_r, q_pos, segs, w_uk, w_uv):
    q_lat = jnp.einsum('bqhd,hrd->bqhr', q_n, w_uk)
    scores = []
    for c_kv, k_r, k_pos in segs:
        s = jnp.einsum('bqhr,bkr->bhqk', q_lat, c_kv) + jnp.einsum('bqhd,bkd->bhqk', q_r, k_r)
        scores.append(causal_mask(s.astype(jnp.float32) * MLA_SCALE, q_pos, k_pos))
    probs = softmax_segments(scores)
    o_lat = None
    for p, (c_kv, _, _) in zip(probs, segs):
        t = jnp.einsum('bhqk,bkr->bqhr', p.astype(c_kv.dtype), c_kv)
        o_lat = t if o_lat is None else o_lat + t
    o = jnp.einsum('bqhr,hrv->bqhv', o_lat, w_uv)
    return o.reshape(o.shape[0], o.shape[1], H_C * DV_C)


def fox_project(h, w_in, b_f):
    B, T, _ = h.shape
    hd = H_D * DH_D
    q, k, v, f = jnp.split(h @ w_in, [hd, 2 * hd, 3 * hd], axis=-1)
    shp = (B, T, H_D, DH_D)
    logf = jax.nn.log_sigmoid((f + b_f).astype(jnp.float32))
    return q.reshape(shp), k.reshape(shp), v.reshape(shp), logf


def fox_attend(q, cq, q_pos, segs):
    B, Tq = q.shape[:2]
    cq_t = jnp.swapaxes(cq, 1, 2).astype(jnp.float32)[..., :, None]
    scores = []
    for k, v, ck, k_pos in segs:
        s = jnp.einsum('bqhd,bkhd->bhqk', q, k).astype(jnp.float32) * FOX_SCALE
        s = s + (cq_t - jnp.swapaxes(ck, 1, 2).astype(jnp.float32)[..., None, :])
        scores.append(causal_mask(s, q_pos, k_pos))
    probs = softmax_segments(scores)
    o = None
    for p, (_, v, _, _) in zip(probs, segs):
        t = jnp.einsum('bhqk,bkhd->bqhd', p.astype(v.dtype), v)
        o = t if o is None else o + t
    return o.reshape(B, Tq, H_D * DH_D)


def setup_inputs(seed: int = 0) -> dict:
    key = jax.random.key(seed)
    keys = iter(jax.random.split(key, 40))
    def nrm(shape, scale):
        return jax.random.normal(next(keys), shape, jnp.float32) * scale
    def gain(shape):
        return 1.0 + nrm(shape, 0.05)
    n_pages = PAST_LEN // PAGE_SIZE
    n_used = DEC_BATCH * n_pages
    n_pool = n_used + max(1, n_used // 4)
    page_table = jax.random.permutation(next(keys), n_pool)[:n_used].reshape(DEC_BATCH, n_pages).astype(jnp.int32)
    D = D_MODEL
    hd = H_D * DH_D
    return {
        'x_prompt': nrm((BATCH, SEQ, D), 1.0),
        'x_sample': nrm((DEC_BATCH, DEC_SEQ, D), 1.0),
        'state_b_buf': nrm((N_LAYERS_B, DEC_BATCH, POOL_HIST, D), 1.0),
        'cache_ckv_c': nrm((N_LAYERS_C, n_pool, PAGE_SIZE, KV_RANK), 1.0),
        'cache_kr_c': nrm((N_LAYERS_C, n_pool, PAGE_SIZE, DR_C), 1.0),
        'cache_k_d': nrm((N_LAYERS_D, n_pool, PAGE_SIZE, H_D, DH_D), 1.0),
        'cache_v_d': nrm((N_LAYERS_D, n_pool, PAGE_SIZE, H_D, DH_D), 1.0),
        'cache_logf_d': jax.nn.log_sigmoid(nrm((N_LAYERS_D, n_pool, PAGE_SIZE, H_D), 1.0) + FORGET_BIAS),
        'page_table': page_table,
        'g_norm': gain((DEPTH, 4, D)),
        'w_up': nrm((DEPTH, D, D_FF), D ** -0.5),
        'w_down': nrm((DEPTH, D_FF, D), D_FF ** -0.5),
        'w_in_a': nrm((N_LAYERS_A, D, 2 * D_A), D ** -0.5),
        'g_v_a': gain((N_LAYERS_A, D_A)),
        'b_v_a': nrm((N_LAYERS_A, D_A), 0.02),
        'w_s_a': nrm((N_LAYERS_A, G_A, CHUNK, CHUNK), CHUNK ** -0.5),
        'b_s_a': 1.0 + nrm((N_LAYERS_A, G_A, CHUNK), 0.05),
        'w_out_a': nrm((N_LAYERS_A, D_A, D), D_A ** -0.5),
        'w_in_b': nrm((N_LAYERS_B, D, D), D ** -0.5),
        'w_grp_b': nrm((N_LAYERS_B, G_B, C_B, C_B), C_B ** -0.5),
        'scale_b': 1.0 + nrm((N_LAYERS_B, D), 0.1),
        'w_in_c': nrm((N_LAYERS_C, D, Q_RANK + KV_RANK + DR_C), D ** -0.5),
        'g_q_c': gain((N_LAYERS_C, Q_RANK)),
        'g_kv_c': gain((N_LAYERS_C, KV_RANK)),
        'w_uq_c': nrm((N_LAYERS_C, Q_RANK, H_C * (DN_C + DR_C)), Q_RANK ** -0.5),
        'w_uk_c': nrm((N_LAYERS_C, H_C, KV_RANK, DN_C), KV_RANK ** -0.5),
        'w_uv_c': nrm((N_LAYERS_C, H_C, KV_RANK, DV_C), KV_RANK ** -0.5),
        'w_o_c': nrm((N_LAYERS_C, H_C * DV_C, D), (H_C * DV_C) ** -0.5),
        'w_in_d': nrm((N_LAYERS_D, D, 3 * hd + H_D), D ** -0.5),
        'b_f_d': FORGET_BIAS + nrm((N_LAYERS_D, H_D), 0.5),
        'w_o_d': nrm((N_LAYERS_D, hd, D), hd ** -0.5),
    }


def reference(x_prompt, x_sample, state_b_buf, cache_ckv_c, cache_kr_c, cache_k_d, cache_v_d, cache_logf_d,
              page_table, g_norm, w_up, w_down, w_in_a, g_v_a, b_v_a, w_s_a, b_s_a, w_out_a,
              w_in_b, w_grp_b, scale_b, w_in_c, g_q_c, g_kv_c, w_uq_c, w_uk_c, w_uv_c, w_o_c,
              w_in_d, b_f_d, w_o_d):
    B, S, _ = x_prompt.shape
    DB, T, _ = x_sample.shape
    past = page_table.shape[1] * PAGE_SIZE
    pos_p = jnp.arange(S, dtype=jnp.int32)
    pos_s = past + jnp.arange(T, dtype=jnp.int32)
    pos_past = jnp.arange(past, dtype=jnp.int32)
    v_a_s, buf_b_p, buf_b_s = [], [], []
    ckv_c_p, kr_c_p, ckv_c_s, kr_c_s = [], [], [], []
    k_d_p, v_d_p, logf_d_p, k_d_s, v_d_s, logf_d_s = [], [], [], [], [], []
    xp, xs = x_prompt, x_sample
    for i in range(DEPTH):
        kind, j = i % N_MIXERS, i // N_MIXERS
        hp = rms_norm(xp, g_norm[i, 0])
        hs = rms_norm(xs, g_norm[i, 0])
        if kind == 0:
            wa = (w_in_a[j], g_v_a[j], b_v_a[j], w_s_a[j], b_s_a[j], w_out_a[j])
            mp, _ = chunk_mlp(hp, *wa)
            ms, v_new = chunk_mlp(hs, *wa)
            v_a_s.append(v_new)
        elif kind == 1:
            wb = (w_in_b[j], w_grp_b[j], scale_b[j])
            hist0 = jnp.zeros((B, POOL_HIST, D_MODEL), hp.dtype)
            mp, hist_p = pool_mix(hp, hist0, pos_p, *wb)
            ms, hist_s = pool_mix(hs, state_b_buf[j], pos_s, *wb)
            buf_b_p.append(hist_p)
            buf_b_s.append(hist_s)
        elif kind == 2:
            wc = (w_in_c[j], g_q_c[j], g_kv_c[j], w_uq_c[j])
            qn, qr, ckv, kr = mla_project(hp, pos_p, *wc)
            segs_p = [(ckv, kr, pos_p)]
            op = sweep_queries(lambda a, b_, c_: mla_attend(a, b_, c_, segs_p, w_uk_c[j], w_uv_c[j]), (qn, qr), pos_p)
            mp = op @ w_o_c[j]
            qn_s, qr_s, ckv_s, kr_s = mla_project(hs, pos_s, *wc)
            segs_s = [(gather_pages(cache_ckv_c, j, page_table), gather_pages(cache_kr_c, j, page_table), pos_past),
                      (ckv_s, kr_s, pos_s)]
            ms = mla_attend(qn_s, qr_s, pos_s, segs_s, w_uk_c[j], w_uv_c[j]) @ w_o_c[j]
            ckv_c_p.append(ckv)
            kr_c_p.append(kr)
            ckv_c_s.append(ckv_s)
            kr_c_s.append(kr_s)
        else:
            q, k, v, logf = fox_project(hp, w_in_d[j], b_f_d[j])
            c = jnp.cumsum(logf, axis=1)
            segs_p = [(k, v, c, pos_p)]
            op = sweep_queries(lambda a, cc, pp: fox_attend(a, cc, pp, segs_p), (q, c), pos_p)
            mp = op @ w_o_d[j]
            q_s, k_s, v_s, logf_s = fox_project(hs, w_in_d[j], b_f_d[j])
            logf_past = gather_pages(cache_logf_d, j, page_table).astype(jnp.float32)
            c_all = jnp.cumsum(jnp.concatenate([logf_past, logf_s], axis=1), axis=1)
            c_past, c_new = c_all[:, :past], c_all[:, past:]
            segs_s = [(gather_pages(cache_k_d, j, page_table), gather_pages(cache_v_d, j, page_table), c_past, pos_past),
                      (k_s, v_s, c_new, pos_s)]
            ms = fox_attend(q_s, c_new, pos_s, segs_s) @ w_o_d[j]
            k_d_p.append(k)
            v_d_p.append(v)
            logf_d_p.append(logf)
            k_d_s.append(k_s)
            v_d_s.append(v_s)
            logf_d_s.append(logf_s)
        xp = xp + rms_norm(mp, g_norm[i, 1])
        xs = xs + rms_norm(ms, g_norm[i, 1])
        xp = xp + rms_norm(sq_relu_mlp(rms_norm(xp, g_norm[i, 2]), w_up[i], w_down[i]), g_norm[i, 3])
        xs = xs + rms_norm(sq_relu_mlp(rms_norm(xs, g_norm[i, 2]), w_up[i], w_down[i]), g_norm[i, 3])
    return (xp, xs,
            jnp.stack(v_a_s),
            jnp.stack(buf_b_p), jnp.stack(buf_b_s),
            jnp.stack(ckv_c_p), jnp.stack(kr_c_p), jnp.stack(ckv_c_s), jnp.stack(kr_c_s),
            jnp.stack(k_d_p), jnp.stack(v_d_p), jnp.stack(logf_d_p),
            jnp.stack(k_d_s), jnp.stack(v_d_s), jnp.stack(logf_d_s))
```

```python
import functools

import jax
import jax.numpy as jnp
from jax import lax
from jax.experimental import pallas as pl
from jax.experimental.pallas import tpu as pltpu

F32 = jnp.float32
BF16 = jnp.bfloat16
EPS = 1e-6
ROPE_THETA = 10000.0
POOL_WINDOWS = (2, 4, 8, 16)
POOL_HIST = max(POOL_WINDOWS) - 1
HIST_ROWS = 16
BF16_ROWS = 16
V7X_VMEM_LIMIT = 56 * 1024 * 1024
PAGES_PER_STEP = 8


def _cparams(*sem):
    return pltpu.CompilerParams(dimension_semantics=sem, vmem_limit_bytes=V7X_VMEM_LIMIT)


def _const_spec(shape):
    nd = len(shape)
    return pl.BlockSpec(shape, lambda *_: (0,) * nd, pipeline_mode=pl.Buffered(1))


def _rms(x, g):
    return x * lax.rsqrt(jnp.mean(x * x, axis=-1, keepdims=True) + EPS) * g


def _dot(a, b):
    return jnp.dot(a, b, preferred_element_type=F32)


def _dot_nt(a, b):
    return lax.dot_general(a, b, (((1,), (1,)), ((), ())), preferred_element_type=F32)


def _split3(x):
    hi = x.astype(BF16)
    r = x - hi.astype(F32)
    mid = r.astype(BF16)
    lo = (r - mid.astype(F32)).astype(BF16)
    return hi, mid, lo


def _cumsum_rows(x):
    n = x.shape[0]
    tri = jnp.where(lax.broadcasted_iota(jnp.int32, (n, n), 1) <= lax.broadcasted_iota(jnp.int32, (n, n), 0),
                    1.0, 0.0).astype(BF16)
    hi, mid, lo = _split3(x)
    return _dot(tri, hi) + _dot(tri, mid) + _dot(tri, lo)


def _online_softmax_step(s, v, m_ref, l_ref, acc_ref):
    m_prev = m_ref[...]
    m_new = jnp.maximum(m_prev, jnp.max(s, axis=1, keepdims=True))
    alpha = jnp.exp(m_prev - m_new)
    p = jnp.exp(s - m_new)
    l_ref[...] = alpha * l_ref[...] + jnp.sum(p, axis=1, keepdims=True)
    acc_ref[...] = alpha * acc_ref[...] + _dot(p.astype(BF16), v)
    m_ref[...] = m_new


def _init_softmax_state(m_ref, l_ref, acc_ref):
    m_ref[...] = jnp.full(m_ref.shape, -jnp.inf, F32)
    l_ref[...] = jnp.zeros(l_ref.shape, F32)
    acc_ref[...] = jnp.zeros(acc_ref.shape, F32)


def _ffn_kernel(*refs, fc, with_proj):
    if with_proj:
        a_ref, x_ref, wo_ref, g_ref, wu_ref, wd_ref, o_ref = refs
        x1 = x_ref[...] + _rms(_dot(a_ref[...], wo_ref[...]), g_ref[1:2, :])
    else:
        x_ref, g_ref, wu_ref, wd_ref, o_ref = refs
        x1 = x_ref[...]
    h = _rms(x1, g_ref[2:3, :]).astype(BF16)
    acc = None
    for c in range(wu_ref.shape[1] // fc):
        u = jnp.maximum(_dot(h, wu_ref[:, c * fc:(c + 1) * fc]), 0.0)
        d = _dot((u * u).astype(BF16), wd_ref[c * fc:(c + 1) * fc, :])
        acc = d if acc is None else acc + d
    o_ref[...] = x1 + _rms(acc, g_ref[3:4, :])


def _ffn(x, g4, wu, wd, a=None, wo=None, tm=512):
    n, d = x.shape
    tm = min(tm, n)
    assert n % tm == 0
    f = wu.shape[1]
    fc = min(1024, f)
    assert f % fc == 0
    row = lambda i: (i, 0)
    specs, args = [], []
    if a is not None:
        specs += [pl.BlockSpec((tm, a.shape[1]), row), pl.BlockSpec((tm, d), row), _const_spec(wo.shape)]
        args += [a, x, wo]
    else:
        specs += [pl.BlockSpec((tm, d), row)]
        args += [x]
    specs += [_const_spec(g4.shape), _const_spec(wu.shape), _const_spec(wd.shape)]
    args += [g4, wu, wd]
    return pl.pallas_call(
        functools.partial(_ffn_kernel, fc=fc, with_proj=a is not None),
        grid=(n // tm,), in_specs=specs, out_specs=pl.BlockSpec((tm, d), row),
        out_shape=jax.ShapeDtypeStruct((n, d), F32), compiler_params=_cparams("parallel"))(*args)


def _gmlp_kernel(x_ref, g_ref, win_ref, gv_ref, bv_ref, ws_ref, bias_ref, wout_ref, o_ref, v_ref, *, chunk):
    x = x_ref[...]
    d_a = wout_ref.shape[0]
    n_groups = ws_ref.shape[0]
    c_a = d_a // n_groups
    uv = jax.nn.gelu(_dot(_rms(x, g_ref[0:1, :]).astype(BF16), win_ref[...]))
    u = uv[:, :d_a]
    v = uv[:, d_a:]
    vc = v - jnp.mean(v, axis=-1, keepdims=True)
    v = vc * lax.rsqrt(jnp.mean(vc * vc, axis=-1, keepdims=True) + EPS) * gv_ref[...] + bv_ref[...]
    v_ref[...] = v
    vb = v.astype(BF16)
    rows = []
    for c in range(x.shape[0] // chunk):
        cols = [_dot(ws_ref[g], vb[c * chunk:(c + 1) * chunk, g * c_a:(g + 1) * c_a]) for g in range(n_groups)]
        rows.append(jnp.concatenate(cols, axis=1))
    mixed = jnp.concatenate(rows, axis=0) + bias_ref[...]
    m = _dot((u * mixed).astype(BF16), wout_ref[...])
    o_ref[...] = x + _rms(m, g_ref[1:2, :])


def _gmlp(x, g4, w_in, g_v, b_v, ws_eff, bias_rows, w_out, chunk, tm=256):
    n, d = x.shape
    tm = min(tm, n)
    assert n % tm == 0 and tm % chunk == 0
    d_a = w_out.shape[0]
    bias = jnp.tile(bias_rows, (tm // chunk, 1))
    row = lambda i: (i, 0)
    return pl.pallas_call(
        functools.partial(_gmlp_kernel, chunk=chunk),
        grid=(n // tm,),
        in_specs=[pl.BlockSpec((tm, d), row), _const_spec(g4.shape), _const_spec(w_in.shape),
                  _const_spec(g_v.shape), _const_spec(b_v.shape), _const_spec(ws_eff.shape),
                  _const_spec(bias.shape), _const_spec(w_out.shape)],
        out_specs=[pl.BlockSpec((tm, d), row), pl.BlockSpec((tm, d_a), row)],
        out_shape=[jax.ShapeDtypeStruct((n, d), F32), jax.ShapeDtypeStruct((n, d_a), F32)],
        compiler_params=_cparams("parallel"))(x, g4, w_in, g_v, b_v, ws_eff, bias, w_out)


def _pool_finish(x, dmat, g_ref, wg_ref, sc_ref):
    n_g = wg_ref.shape[0]
    cb = x.shape[1] // n_g
    ys = [_dot(dmat[:, gi * cb:(gi + 1) * cb], wg_ref[gi]) for gi in range(n_g)]
    y = jnp.concatenate(ys, axis=1) * sc_ref[...]
    return x + _rms(y, g_ref[1:2, :])


def _pool_prompt_kernel(x_ref, g_ref, win_ref, wg_ref, sc_ref, o_ref, buf_ref, zbuf, *, tiles_per_seq):
    tm, d = x_ref.shape
    li = pl.program_id(0) % tiles_per_seq

    @pl.when(li == 0)
    def _():
        zbuf[0:HIST_ROWS, :] = jnp.zeros((HIST_ROWS, d), F32)

    x = x_ref[...]
    z = _dot(_rms(x, g_ref[0:1, :]).astype(BF16), win_ref[...])
    zbuf[HIST_ROWS:HIST_ROWS + tm, :] = z
    pos = li * tm + lax.broadcasted_iota(jnp.int32, (tm, 1), 0)
    cb = d // len(POOL_WINDOWS)
    cols = []
    for gi, w in enumerate(POOL_WINDOWS):
        sl = slice(gi * cb, (gi + 1) * cb)
        zg = z[:, sl]
        acc = zg
        for k in range(1, w):
            acc = acc + zbuf[HIST_ROWS - k:HIST_ROWS - k + tm, sl]
        cnt = jnp.minimum(w, pos + 1).astype(F32)
        cols.append(acc / cnt - zg)
    dmat = jnp.concatenate(cols, axis=1).astype(BF16)
    o_ref[...] = _pool_finish(x, dmat, g_ref, wg_ref, sc_ref)
    tail = z[tm - HIST_ROWS:, :]
    zbuf[0:HIST_ROWS, :] = tail
    buf_ref[0] = tail


def _pool_prompt(x, g4, w_in, w_grp, scale, n_seq, tm=256):
    n, d = x.shape
    s = n // n_seq
    tm = min(tm, s)
    assert s % tm == 0 and tm >= HIST_ROWS
    tps = s // tm
    row = lambda i: (i, 0)
    return pl.pallas_call(
        functools.partial(_pool_prompt_kernel, tiles_per_seq=tps),
        grid=(n // tm,),
        in_specs=[pl.BlockSpec((tm, d), row), _const_spec(g4.shape), _const_spec(w_in.shape),
                  _const_spec(w_grp.shape), _const_spec(scale.shape)],
        out_specs=[pl.BlockSpec((tm, d), row), pl.BlockSpec((1, HIST_ROWS, d), lambda i: (i // tps, 0, 0))],
        out_shape=[jax.ShapeDtypeStruct((n, d), F32), jax.ShapeDtypeStruct((n_seq, HIST_ROWS, d), F32)],
        scratch_shapes=[pltpu.VMEM((HIST_ROWS + tm, d), F32)],
        compiler_params=_cparams("arbitrary"))(x, g4, w_in, w_grp, scale)


def _pool_sample_kernel(x_ref, hist_ref, g_ref, win_ref, wg_ref, sc_ref, o_ref, z_ref, *, db, t_new, pos0):
    x = x_ref[...]
    d = x.shape[1]
    z = _dot(_rms(x, g_ref[0:1, :]).astype(BF16), win_ref[...])
    z_ref[...] = z
    cb = d // len(POOL_WINDOWS)

    def rows_at(p, sl):
        if p >= POOL_HIST:
            return z[(p - POOL_HIST) * db:(p - POOL_HIST + 1) * db, sl]
        return hist_ref[p * db:(p + 1) * db, sl]

    drows = []
    for t in range(t_new):
        cols = []
        for gi, w in enumerate(POOL_WINDOWS):
            sl = slice(gi * cb, (gi + 1) * cb)
            acc = rows_at(POOL_HIST + t, sl)
            for k in range(1, w):
                acc = acc + rows_at(POOL_HIST + t - k, sl)
            cnt = float(min(w, pos0 + t + 1))
            cols.append(acc / cnt - z[t * db:(t + 1) * db, sl])
        drows.append(jnp.concatenate(cols, axis=1))
    dmat = jnp.concatenate(drows, axis=0).astype(BF16)
    o_ref[...] = _pool_finish(x, dmat, g_ref, wg_ref, sc_ref)


def _pool_sample(x_tm, hist_tm, g4, w_in, w_grp, scale, db, t_new, pos0):
    n, d = x_tm.shape
    args = (x_tm, hist_tm, g4, w_in, w_grp, scale)
    return pl.pallas_call(
        functools.partial(_pool_sample_kernel, db=db, t_new=t_new, pos0=pos0),
        grid=(1,),
        in_specs=[_const_spec(a.shape) for a in args],
        out_specs=[pl.BlockSpec((n, d), lambda i: (0, 0)), pl.BlockSpec((n, d), lambda i: (0, 0))],
        out_shape=[jax.ShapeDtypeStruct((n, d), F32), jax.ShapeDtypeStruct((n, d), F32)],
        compiler_params=_cparams("arbitrary"))(*args)


def _mla_proj_kernel(x_ref, g_ref, win_ref, gq_ref, gkv_ref, wuq_ref, wukt_ref, cos_ref, sin_ref,
                     q_ref, kcat_ref, ckv_ref, kr_ref, *, q_rank, kv_rank, scale):
    n_heads, dn, _ = wukt_ref.shape
    x = x_ref[...]
    c = _dot(_rms(x, g_ref[0:1, :]).astype(BF16), win_ref[...])
    c_q = c[:, :q_rank]
    c_kv = c[:, q_rank:q_rank + kv_rank]
    k_r = c[:, q_rank + kv_rank:]
    q = _dot(_rms(c_q, gq_ref[...]).astype(BF16), wuq_ref[...])
    hd = n_heads * dn
    half = (q.shape[1] - hd) // 2
    hr = half // n_heads
    x1 = q[:, hd:hd + half]
    x2 = q[:, hd + half:]
    cosq = cos_ref[...]
    sinq = sin_ref[...]
    r1 = ((x1 * cosq - x2 * sinq) * scale).astype(BF16)
    r2 = ((x1 * sinq + x2 * cosq) * scale).astype(BF16)
    ckv = _rms(c_kv, gkv_ref[...])
    ckv_ref[...] = ckv
    k1 = k_r[:, :hr]
    k2 = k_r[:, hr:]
    c1 = cosq[:, :hr]
    s1 = sinq[:, :hr]
    kr = jnp.concatenate([k1 * c1 - k2 * s1, k1 * s1 + k2 * c1], axis=1)
    kr_ref[...] = kr
    kcat_ref[:, :kv_rank] = ckv.astype(BF16)
    kcat_ref[:, kv_rank:] = kr.astype(BF16)
    for h in range(n_heads):
        ql = _dot(q[:, h * dn:(h + 1) * dn].astype(BF16), wukt_ref[h]) * scale
        q_ref[h, :, :kv_rank] = ql.astype(BF16)
        q_ref[h, :, kv_rank:kv_rank + hr] = r1[:, h * hr:(h + 1) * hr]
        q_ref[h, :, kv_rank + hr:] = r2[:, h * hr:(h + 1) * hr]


def _mla_proj(x, g4, w_in, g_q, g_kv, wuq_perm, wukt, cosq, sinq, scale, tm=256):
    n, d = x.shape
    tm = min(tm, n)
    assert n % tm == 0
    n_heads, _, kv_rank = wukt.shape
    q_rank = g_q.shape[1]
    dr = w_in.shape[1] - q_rank - kv_rank
    dk = kv_rank + dr
    row = lambda i: (i, 0)
    return pl.pallas_call(
        functools.partial(_mla_proj_kernel, q_rank=q_rank, kv_rank=kv_rank, scale=scale),
        grid=(n // tm,),
        in_specs=[pl.BlockSpec((tm, d), row), _const_spec(g4.shape), _const_spec(w_in.shape),
                  _const_spec(g_q.shape), _const_spec(g_kv.shape), _const_spec(wuq_perm.shape),
                  _const_spec(wukt.shape), pl.BlockSpec((tm, cosq.shape[1]), row),
                  pl.BlockSpec((tm, sinq.shape[1]), row)],
        out_specs=[pl.BlockSpec((n_heads, tm, dk), lambda i: (0, i, 0)), pl.BlockSpec((tm, dk), row),
                   pl.BlockSpec((tm, kv_rank), row), pl.BlockSpec((tm, dr), row)],
        out_shape=[jax.ShapeDtypeStruct((n_heads, n, dk), BF16), jax.ShapeDtypeStruct((n, dk), BF16),
                   jax.ShapeDtypeStruct((n, kv_rank), F32), jax.ShapeDtypeStruct((n, dr), F32)],
        compiler_params=_cparams("parallel"))(x, g4, w_in, g_q, g_kv, wuq_perm, wukt, cosq, sinq)


def _mla_flash_kernel(q_ref, k_ref, wuv_ref, o_ref, m_ref, l_ref, acc_ref, *, tq, tk, kv_rank):
    i = pl.program_id(1)
    j = pl.program_id(2)
    n_heads = q_ref.shape[0]
    m_rows = n_heads * tq

    @pl.when(j == 0)
    def _():
        _init_softmax_state(m_ref, l_ref, acc_ref)

    @pl.when(j * tk <= i * tq + tq - 1)
    def _():
        q = q_ref[...].reshape(m_rows, q_ref.shape[2])
        k = k_ref[...]
        s = _dot_nt(q, k)
        q_pos = i * tq + (lax.broadcasted_iota(jnp.int32, (m_rows, 1), 0) & (tq - 1))
        k_pos = j * tk + lax.broadcasted_iota(jnp.int32, (1, tk), 1)
        s = jnp.where(k_pos <= q_pos, s, -jnp.inf)
        _online_softmax_step(s, k[:, :kv_rank], m_ref, l_ref, acc_ref)

    @pl.when(j == pl.num_programs(2) - 1)
    def _():
        ol = (acc_ref[...] / l_ref[...]).astype(BF16)
        outs = [_dot(ol[h * tq:(h + 1) * tq, :], wuv_ref[h]) for h in range(n_heads)]
        o_ref[...] = jnp.concatenate(outs, axis=1).astype(o_ref.dtype)


def _mla_flash(q_cat, kcat, wuv, n_seq, tq=128, tk=512):
    n_heads, n, dk = q_cat.shape
    s = n // n_seq
    tq = min(tq, s)
    tk = min(tk, s)
    assert s % tq == 0 and s % tk == 0 and tq & (tq - 1) == 0
    nq, nk = s // tq, s // tk
    kv_rank, dv = wuv.shape[1], wuv.shape[2]
    last_k = lambda i: (i * tq + tq - 1) // tk
    return pl.pallas_call(
        functools.partial(_mla_flash_kernel, tq=tq, tk=tk, kv_rank=kv_rank),
        grid=(n_seq, nq, nk),
        in_specs=[pl.BlockSpec((n_heads, tq, dk), lambda b, i, j: (0, b * nq + i, 0)),
                  pl.BlockSpec((tk, dk), lambda b, i, j: (b * nk + jnp.minimum(j, last_k(i)), 0)),
                  _const_spec(wuv.shape)],
        out_specs=pl.BlockSpec((tq, n_heads * dv), lambda b, i, j: (b * nq + i, 0)),
        out_shape=jax.ShapeDtypeStruct((n, n_heads * dv), BF16),
        scratch_shapes=[pltpu.VMEM((n_heads * tq, 1), F32), pltpu.VMEM((n_heads * tq, 1), F32),
                        pltpu.VMEM((n_heads * tq, kv_rank), F32)],
        compiler_params=_cparams("parallel", "parallel", "arbitrary"))(q_cat, kcat, wuv)


def _head_diag(full, n_heads, t_new):
    width = full.shape[1]
    dv = width // n_heads
    own = (lax.broadcasted_iota(jnp.int32, (n_heads, width), 1) // dv
           == lax.broadcasted_iota(jnp.int32, (n_heads, width), 0))
    full3 = full.reshape(t_new, n_heads, width)
    return jnp.sum(jnp.where(own[None], full3, 0.0), axis=1)


def _mla_decode_kernel(pt_ref, q_ref, knew_ref, wuv_ref, *rest, n_pp, page, kv_rank, t_new, n_heads):
    ckv_refs = rest[:n_pp]
    kr_refs = rest[n_pp:2 * n_pp]
    o_ref = rest[2 * n_pp]
    m_ref, l_ref, acc_ref, kbuf = rest[2 * n_pp + 1:]
    j = pl.program_id(1)

    @pl.when(j == 0)
    def _():
        _init_softmax_state(m_ref, l_ref, acc_ref)

    for r in range(n_pp):
        kbuf[r * page:(r + 1) * page, :kv_rank] = ckv_refs[r][0, 0].astype(BF16)
        kbuf[r * page:(r + 1) * page, kv_rank:] = kr_refs[r][0, 0].astype(BF16)
    q = q_ref[...]
    kb = kbuf[...]
    _online_softmax_step(_dot_nt(q, kb), kb[:, :kv_rank], m_ref, l_ref, acc_ref)

    @pl.when(j == pl.num_programs(1) - 1)
    def _():
        kn = knew_ref[0]
        s = _dot_nt(q, kn)
        q_t = lax.broadcasted_iota(jnp.int32, (s.shape[0], 1), 0) // n_heads
        k_t = lax.broadcasted_iota(jnp.int32, (1, s.shape[1]), 1)
        s = jnp.where(k_t <= q_t, s, -jnp.inf)
        _online_softmax_step(s, kn[:, :kv_rank], m_ref, l_ref, acc_ref)
        ol = (acc_ref[...] / l_ref[...]).astype(BF16)
        o_ref[0] = _head_diag(_dot(ol, wuv_ref[...]), n_heads, t_new).astype(o_ref.dtype)


def _mla_decode(q_rows, knew, wuv_all, cache_ckv, cache_kr, layer, page_table, t_new, n_heads):
    db, n_pages = page_table.shape
    page, kv_rank = cache_ckv.shape[2], cache_ckv.shape[3]
    dr = cache_kr.shape[3]
    dk = kv_rank + dr
    n_pp = min(PAGES_PER_STEP, n_pages)
    assert n_pages % n_pp == 0
    rows = t_new * n_heads
    width = wuv_all.shape[1]

    def page_spec(r, last):
        return pl.BlockSpec((1, 1, page, last), lambda b, j, pt: (layer, pt[b, j * n_pp + r], 0, 0))

    in_specs = ([pl.BlockSpec((rows, dk), lambda b, j, pt: (b, 0)),
                 pl.BlockSpec((1, BF16_ROWS, dk), lambda b, j, pt: (b, 0, 0)),
                 _const_spec(wuv_all.shape)]
                + [page_spec(r, kv_rank) for r in range(n_pp)] + [page_spec(r, dr) for r in range(n_pp)])
    grid_spec = pltpu.PrefetchScalarGridSpec(
        num_scalar_prefetch=1, grid=(db, n_pages // n_pp), in_specs=in_specs,
        out_specs=pl.BlockSpec((1, t_new, width), lambda b, j, pt: (b, 0, 0)),
        scratch_shapes=[pltpu.VMEM((rows, 1), F32), pltpu.VMEM((rows, 1), F32), pltpu.VMEM((rows, kv_rank), F32),
                        pltpu.VMEM((n_pp * page, dk), BF16)])
    return pl.pallas_call(
        functools.partial(_mla_decode_kernel, n_pp=n_pp, page=page, kv_rank=kv_rank, t_new=t_new, n_heads=n_heads),
        grid_spec=grid_spec, out_shape=jax.ShapeDtypeStruct((db, t_new, width), BF16),
        compiler_params=_cparams("parallel", "arbitrary"))(
            page_table, q_rows, knew, wuv_all, *([cache_ckv] * n_pp), *([cache_kr] * n_pp))


def _fox_proj_kernel(x_ref, g_ref, win_ref, bf_ref, q_ref, k_ref, v_ref, kb_ref, vb_ref, lf_ref, *rest,
                     hd, scale, tiles_per_seq):
    x = x_ref[...]
    y = _dot(_rms(x, g_ref[0:1, :]).astype(BF16), win_ref[...])
    q_ref[...] = (y[:, :hd] * scale).astype(BF16)
    k = y[:, hd:2 * hd]
    v = y[:, 2 * hd:3 * hd]
    k_ref[...] = k
    v_ref[...] = v
    kb_ref[...] = k.astype(BF16)
    vb_ref[...] = v.astype(BF16)
    lf = jax.nn.log_sigmoid(y[:, 3 * hd:] + bf_ref[...])
    lf_ref[...] = lf
    if tiles_per_seq is not None:
        c_ref, carry = rest

        @pl.when(pl.program_id(0) % tiles_per_seq == 0)
        def _():
            carry[...] = jnp.zeros(carry.shape, F32)

        c = _cumsum_rows(lf) + carry[...]
        c_ref[...] = c
        carry[...] = c[c.shape[0] - 1:, :]


def _fox_proj(x, g4, w_in, b_f, scale, n_seq=None, tm=256):
    n, d = x.shape
    n_heads = b_f.shape[1]
    hd = (w_in.shape[1] - n_heads) // 3
    s = n // n_seq if n_seq else n
    tm = min(tm, s)
    assert s % tm == 0
    row = lambda i: (i, 0)
    wide = pl.BlockSpec((tm, hd), row)
    narrow = pl.BlockSpec((tm, n_heads), row)
    out_specs = [wide, wide, wide, wide, wide, narrow]
    out_shape = [jax.ShapeDtypeStruct((n, hd), BF16), jax.ShapeDtypeStruct((n, hd), F32),
                 jax.ShapeDtypeStruct((n, hd), F32), jax.ShapeDtypeStruct((n, hd), BF16),
                 jax.ShapeDtypeStruct((n, hd), BF16), jax.ShapeDtypeStruct((n, n_heads), F32)]
    scratch = []
    if n_seq:
        out_specs.append(narrow)
        out_shape.append(jax.ShapeDtypeStruct((n, n_heads), F32))
        scratch.append(pltpu.VMEM((1, n_heads), F32))
    return pl.pallas_call(
        functools.partial(_fox_proj_kernel, hd=hd, scale=scale, tiles_per_seq=(s // tm) if n_seq else None),
        grid=(n // tm,),
        in_specs=[pl.BlockSpec((tm, d), row), _const_spec(g4.shape), _const_spec(w_in.shape),
                  _const_spec(b_f.shape)],
        out_specs=out_specs, out_shape=out_shape, scratch_shapes=scratch,
        compiler_params=_cparams("arbitrary"))(x, g4, w_in, b_f)


def _fox_flash_kernel(q_ref, k_ref, v_ref, cq_ref, ck_ref, o_ref, m_ref, l_ref, acc_ref, *, tq, tk, dh):
    i = pl.program_id(2)
    j = pl.program_id(3)
    width = q_ref.shape[1]
    n_sub = width // dh
    lane_head = lax.broadcasted_iota(jnp.int32, (1, width), 1) // dh

    @pl.when(j == 0)
    def _():
        _init_softmax_state(m_ref, l_ref, acc_ref)

    @pl.when(j * tk <= i * tq + tq - 1)
    def _():
        q = q_ref[...]
        k = k_ref[...]
        v = v_ref[...]
        cq = cq_ref[0, 0]
        ck = ck_ref[0, 0]
        q_pos = i * tq + lax.broadcasted_iota(jnp.int32, (tq, 1), 0)
        k_pos = j * tk + lax.broadcasted_iota(jnp.int32, (1, tk), 1)
        causal = k_pos <= q_pos
        alpha_full = None
        pv_full = None
        for hh in range(n_sub):
            qm = jnp.where(lane_head == hh, q, jnp.zeros_like(q))
            s = _dot_nt(qm, k) + (cq[:, hh:hh + 1] - ck[hh:hh + 1, :])
            s = jnp.where(causal, s, -jnp.inf)
            m_prev = m_ref[hh]
            m_new = jnp.maximum(m_prev, jnp.max(s, axis=1, keepdims=True))
            alpha = jnp.exp(m_prev - m_new)
            p = jnp.exp(s - m_new)
            l_ref[hh] = alpha * l_ref[hh] + jnp.sum(p, axis=1, keepdims=True)
            m_ref[hh] = m_new
            pv = _dot(p.astype(BF16), v)
            alpha_b = jnp.broadcast_to(alpha, (tq, width))
            alpha_full = alpha_b if alpha_full is None else jnp.where(lane_head == hh, alpha_b, alpha_full)
            pv_full = pv if pv_full is None else jnp.where(lane_head == hh, pv, pv_full)
        acc_ref[...] = alpha_full * acc_ref[...] + pv_full

    @pl.when(j == pl.num_programs(3) - 1)
    def _():
        l_full = None
        for hh in range(n_sub):
            l_b = jnp.broadcast_to(l_ref[hh], (tq, width))
            l_full = l_b if l_full is None else jnp.where(lane_head == hh, l_b, l_full)
        o_ref[...] = (acc_ref[...] / l_full).astype(o_ref.dtype)


def _fox_flash(q, kb, vb, cq, ck, n_seq, dh, tq=512, tk=512, lanes=128):
    n, hd = q.shape
    s = n // n_seq
    tq = min(tq, s)
    tk = min(tk, s)
    assert s % tq == 0 and s % tk == 0 and hd % lanes == 0 and lanes % dh == 0
    nq, nk = s // tq, s // tk
    n_sub = lanes // dh
    last_k = lambda i: (i * tq + tq - 1) // tk
    kv_spec = pl.BlockSpec((tk, lanes), lambda b, g, i, j: (b * nk + jnp.minimum(j, last_k(i)), g))
    return pl.pallas_call(
        functools.partial(_fox_flash_kernel, tq=tq, tk=tk, dh=dh),
        grid=(n_seq, hd // lanes, nq, nk),
        in_specs=[pl.BlockSpec((tq, lanes), lambda b, g, i, j: (b * nq + i, g)), kv_spec, kv_spec,
                  pl.BlockSpec((1, 1, tq, n_sub), lambda b, g, i, j: (b, g, i, 0)),
                  pl.BlockSpec((1, 1, n_sub, tk), lambda b, g, i, j: (b, g, 0, jnp.minimum(j, last_k(i))))],
        out_specs=pl.BlockSpec((tq, lanes), lambda b, g, i, j: (b * nq + i, g)),
        out_shape=jax.ShapeDtypeStruct((n, hd), BF16),
        scratch_shapes=[pltpu.VMEM((n_sub, tq, 1), F32), pltpu.VMEM((n_sub, tq, 1), F32),
                        pltpu.VMEM((tq, lanes), F32)],
        compiler_params=_cparams("parallel", "parallel", "parallel", "arbitrary"))(q, kb, vb, cq, ck)


def _fox_cpast_kernel(pt_ref, lfs_ref, *rest, n_pp, page):
    lf_refs = rest[:n_pp]
    c_ref, cnew_ref, carry = rest[n_pp:]
    j = pl.program_id(1)

    @pl.when(j == 0)
    def _():
        carry[...] = jnp.zeros(carry.shape, F32)

    cur = carry[...]
    for r in range(n_pp):
        c = _cumsum_rows(lf_refs[r][0, 0]) + cur
        c_ref[0, r * page:(r + 1) * page, :] = c
        cur = c[page - 1:, :]
    carry[...] = cur

    @pl.when(j == pl.num_programs(1) - 1)
    def _():
        cnew_ref[0] = _cumsum_rows(lfs_ref[0]) + cur


def _fox_cpast(cache_logf, layer, page_table, lf_new_pad):
    db, n_pages = page_table.shape
    page, n_heads = cache_logf.shape[2], cache_logf.shape[3]
    n_pp = min(2 * PAGES_PER_STEP, n_pages)
    assert n_pages % n_pp == 0
    pad_rows = lf_new_pad.shape[1]

    def page_spec(r):
        return pl.BlockSpec((1, 1, page, n_heads), lambda b, j, pt: (layer, pt[b, j * n_pp + r], 0, 0))

    grid_spec = pltpu.PrefetchScalarGridSpec(
        num_scalar_prefetch=1, grid=(db, n_pages // n_pp),
        in_specs=[pl.BlockSpec((1, pad_rows, n_heads), lambda b, j, pt: (b, 0, 0))]
        + [page_spec(r) for r in range(n_pp)],
        out_specs=[pl.BlockSpec((1, n_pp * page, n_heads), lambda b, j, pt: (b, j, 0)),
                   pl.BlockSpec((1, pad_rows, n_heads), lambda b, j, pt: (b, 0, 0))],
        scratch_shapes=[pltpu.VMEM((1, n_heads), F32)])
    return pl.pallas_call(
        functools.partial(_fox_cpast_kernel, n_pp=n_pp, page=page),
        grid_spec=grid_spec,
        out_shape=[jax.ShapeDtypeStruct((db, n_pages * page, n_heads), F32),
                   jax.ShapeDtypeStruct((db, pad_rows, n_heads), F32)],
        compiler_params=_cparams("parallel", "arbitrary"))(page_table, lf_new_pad, *([cache_logf] * n_pp))


def _fox_decode_kernel(pt_ref, q_ref, cq_ref, knew_ref, vnew_ref, cnew_ref, cpast_ref, *rest,
                       n_pp, page, n_heads, t_new):
    k_refs = rest[:n_pp]
    v_refs = rest[n_pp:2 * n_pp]
    o_ref = rest[2 * n_pp]
    m_ref, l_ref, acc_ref = rest[2 * n_pp + 1:]
    j = pl.program_id(1)
    hd = q_ref.shape[2]
    rows = t_new * n_heads
    dh = hd // n_heads

    @pl.when(j == 0)
    def _():
        _init_softmax_state(m_ref, l_ref, acc_ref)

    own = (lax.broadcasted_iota(jnp.int32, (n_heads, hd), 1) // dh
           == lax.broadcasted_iota(jnp.int32, (n_heads, hd), 0))
    q = q_ref[0]
    qb = jnp.where(own[None], jnp.broadcast_to(q[:, None, :], (t_new, n_heads, hd)), 0.0)
    qb = qb.reshape(rows, hd).astype(BF16)
    pick = jnp.where((lax.broadcasted_iota(jnp.int32, (rows, 3 * n_heads), 0) % n_heads)
                     == (lax.broadcasted_iota(jnp.int32, (rows, 3 * n_heads), 1) % n_heads), 1.0, 0.0).astype(BF16)
    cq = cq_ref[0]

    def key_bias(c):
        return _dot_nt(pick, jnp.concatenate(_split3(c), axis=1))

    s = jnp.concatenate([_dot_nt(qb, k_refs[r][0, 0].astype(BF16)) for r in range(n_pp)], axis=1)
    s = s + (cq - key_bias(cpast_ref[0]))
    m_prev = m_ref[...]
    m_new = jnp.maximum(m_prev, jnp.max(s, axis=1, keepdims=True))
    alpha = jnp.exp(m_prev - m_new)
    p = jnp.exp(s - m_new)
    l_ref[...] = alpha * l_ref[...] + jnp.sum(p, axis=1, keepdims=True)
    m_ref[...] = m_new
    pb = p.astype(BF16)
    pv = None
    for r in range(n_pp):
        t = _dot(pb[:, r * page:(r + 1) * page], v_refs[r][0, 0].astype(BF16))
        pv = t if pv is None else pv + t
    acc_ref[...] = alpha * acc_ref[...] + pv

    @pl.when(j == pl.num_programs(1) - 1)
    def _():
        kn = knew_ref[0]
        s2 = _dot_nt(qb, kn) + (cq - key_bias(cnew_ref[0]))
        q_t = lax.broadcasted_iota(jnp.int32, (rows, 1), 0) // n_heads
        k_t = lax.broadcasted_iota(jnp.int32, (1, s2.shape[1]), 1)
        s2 = jnp.where(k_t <= q_t, s2, -jnp.inf)
        _online_softmax_step(s2, vnew_ref[0], m_ref, l_ref, acc_ref)
        o_ref[0] = _head_diag(acc_ref[...] / l_ref[...], n_heads, t_new).astype(o_ref.dtype)


def _fox_decode(q_s, cq_rows, knew, vnew, cnew, cpast, cache_k, cache_v, layer, page_table, n_heads):
    db, n_pages = page_table.shape
    page, hd = cache_k.shape[2], cache_k.shape[3]
    t_new = q_s.shape[1]
    n_pp = min(PAGES_PER_STEP, n_pages)
    assert n_pages % n_pp == 0
    rows = t_new * n_heads
    per_b = lambda b, j, pt: (b, 0, 0)

    def page_spec(r):
        return pl.BlockSpec((1, 1, page, hd), lambda b, j, pt: (layer, pt[b, j * n_pp + r], 0, 0))

    in_specs = ([pl.BlockSpec((1, t_new, hd), per_b), pl.BlockSpec((1, rows, 1), per_b),
                 pl.BlockSpec((1, BF16_ROWS, hd), per_b), pl.BlockSpec((1, BF16_ROWS, hd), per_b),
                 pl.BlockSpec((1, BF16_ROWS, n_heads), per_b),
                 pl.BlockSpec((1, n_pp * page, n_heads), lambda b, j, pt: (b, j, 0))]
                + [page_spec(r) for r in range(n_pp)] * 2)
    grid_spec = pltpu.PrefetchScalarGridSpec(
        num_scalar_prefetch=1, grid=(db, n_pages // n_pp), in_specs=in_specs,
        out_specs=pl.BlockSpec((1, t_new, hd), per_b),
        scratch_shapes=[pltpu.VMEM((rows, 1), F32), pltpu.VMEM((rows, 1), F32), pltpu.VMEM((rows, hd), F32)])
    return pl.pallas_call(
        functools.partial(_fox_decode_kernel, n_pp=n_pp, page=page, n_heads=n_heads, t_new=t_new),
        grid_spec=grid_spec, out_shape=jax.ShapeDtypeStruct((db, t_new, hd), BF16),
        compiler_params=_cparams("parallel", "arbitrary"))(
            page_table, q_s, cq_rows, knew, vnew, cnew, cpast, *([cache_k] * n_pp), *([cache_v] * n_pp))


def _pad_rows(a, rows):
    return jnp.pad(a, ((0, 0), (0, rows - a.shape[1]), (0, 0)))


def _rope_tables(pos, dr, n_heads):
    inv = ROPE_THETA ** (-jnp.arange(0, dr, 2, dtype=F32) / dr)
    ang = pos.astype(F32)[:, None] * inv[None, :]
    return jnp.tile(jnp.cos(ang), (1, n_heads)), jnp.tile(jnp.sin(ang), (1, n_heads))


def kernel(x_prompt, x_sample, state_b_buf, cache_ckv_c, cache_kr_c, cache_k_d, cache_v_d, cache_logf_d, page_table, g_norm, w_up, w_down, w_in_a, g_v_a, b_v_a, w_s_a, b_s_a, w_out_a, w_in_b, w_grp_b, scale_b, w_in_c, g_q_c, g_kv_c, w_uq_c, w_uk_c, w_uv_c, w_o_c, w_in_d, b_f_d, w_o_d):
    B, S, D = x_prompt.shape
    DB, T, _ = x_sample.shape
    depth = g_norm.shape[0]
    n_pages = page_table.shape[1]
    page = cache_ckv_c.shape[2]
    past = n_pages * page
    pos_p = jnp.arange(S, dtype=jnp.int32)
    pos_s = past + jnp.arange(T, dtype=jnp.int32)
    bf = lambda w: w.astype(BF16)

    xp = x_prompt.reshape(B * S, D)
    xs = x_sample.reshape(DB * T, D)
    outs = {k: [] for k in ("v_a_s", "buf_b_p", "buf_b_s", "ckv_c_p", "kr_c_p", "ckv_c_s", "kr_c_s",
                            "k_d_p", "v_d_p", "logf_d_p", "k_d_s", "v_d_s", "logf_d_s")}
    for i in range(depth):
        kind, j = i % 4, i // 4
        g4 = g_norm[i]
        wu, wd = bf(w_up[i]), bf(w_down[i])
        if kind == 0:
            n_groups, chunk = w_s_a.shape[1], w_s_a.shape[2]
            d_a = w_out_a.shape[1]
            c_a = d_a // n_groups
            assert chunk % T == 0
            tril = jnp.tril(jnp.ones((chunk, chunk), bool))
            ws_p = bf(jnp.where(tril[None], w_s_a[j], 0.0))
            bias_p = jnp.repeat(b_s_a[j].T, c_a, axis=1)
            tril_t = jnp.tril(jnp.ones((T, T), bool))
            ws_t = jnp.where(tril_t[None], w_s_a[j][:, :T, :T], 0.0)
            eye = jnp.eye(chunk // T, dtype=F32)
            ws_s = bf(jnp.einsum("ab,gts->gatbs", eye, ws_t).reshape(n_groups, chunk, chunk))
            bias_s = jnp.tile(jnp.repeat(b_s_a[j][:, :T].T, c_a, axis=1), (chunk // T, 1))
            wa = (g4, bf(w_in_a[j]), g_v_a[j][None], b_v_a[j][None])
            xp, _ = _gmlp(xp, *wa, ws_p, bias_p, bf(w_out_a[j]), chunk)
            xs, v_new = _gmlp(xs, *wa, ws_s, bias_s, bf(w_out_a[j]), chunk)
            outs["v_a_s"].append(v_new.reshape(DB, T, d_a))
            xp = _ffn(xp, g4, wu, wd)
            xs = _ffn(xs, g4, wu, wd)
        elif kind == 1:
            wb = (g4, bf(w_in_b[j]), bf(w_grp_b[j]), scale_b[j][None])
            xp, buf_p = _pool_prompt(xp, *wb, n_seq=B)
            outs["buf_b_p"].append(buf_p[:, HIST_ROWS - POOL_HIST:, :])
            xs_tm = xs.reshape(DB, T, D).transpose(1, 0, 2).reshape(T * DB, D)
            hist_tm = state_b_buf[j].transpose(1, 0, 2).reshape(POOL_HIST * DB, D)
            xs_tm, z_tm = _pool_sample(xs_tm, hist_tm, *wb, db=DB, t_new=T, pos0=past)
            xs = xs_tm.reshape(T, DB, D).transpose(1, 0, 2).reshape(DB * T, D)
            z_s = z_tm.reshape(T, DB, D).transpose(1, 0, 2)
            outs["buf_b_s"].append(jnp.concatenate([state_b_buf[j], z_s], axis=1)[:, -POOL_HIST:])
            xp = _ffn(xp, g4, wu, wd)
            xs = _ffn(xs, g4, wu, wd)
        elif kind == 2:
            n_heads, kv_rank, dn = w_uk_c.shape[1:]
            dv = w_uv_c.shape[3]
            dr = cache_kr_c.shape[3]
            q_rank = g_q_c.shape[1]
            scale = float((dn + dr) ** -0.5)
            wuq3 = w_uq_c[j].reshape(q_rank, n_heads, dn + dr)
            wuq_perm = bf(jnp.concatenate([wuq3[:, :, :dn].reshape(q_rank, -1),
                                           wuq3[:, :, dn:dn + dr // 2].reshape(q_rank, -1),
                                           wuq3[:, :, dn + dr // 2:].reshape(q_rank, -1)], axis=1))
            wukt = bf(w_uk_c[j].transpose(0, 2, 1))
            wuv = bf(w_uv_c[j])
            wuv_all = bf(w_uv_c[j].transpose(1, 0, 2).reshape(kv_rank, n_heads * dv))
            wc = (g4, bf(w_in_c[j]), g_q_c[j][None], g_kv_c[j][None], wuq_perm, wukt)
            cos_p, sin_p = _rope_tables(jnp.tile(pos_p, B), dr, n_heads)
            q_cat, kcat, ckv, kr = _mla_proj(xp, *wc, cos_p, sin_p, scale)
            o_p = _mla_flash(q_cat, kcat, wuv, n_seq=B)
            outs["ckv_c_p"].append(ckv.reshape(B, S, kv_rank))
            outs["kr_c_p"].append(kr.reshape(B, S, dr))
            cos_s, sin_s = _rope_tables(jnp.tile(pos_s, DB), dr, n_heads)
            q_cat_s, kcat_s, ckv_s, kr_s = _mla_proj(xs, *wc, cos_s, sin_s, scale)
            dk = kv_rank + dr
            q_rows = q_cat_s.reshape(n_heads, DB, T, dk).transpose(1, 2, 0, 3).reshape(DB * T * n_heads, dk)
            knew = _pad_rows(kcat_s.reshape(DB, T, dk), BF16_ROWS)
            o_s = _mla_decode(q_rows, knew, wuv_all, cache_ckv_c, cache_kr_c, j, page_table, T, n_heads)
            outs["ckv_c_s"].append(ckv_s.reshape(DB, T, kv_rank))
            outs["kr_c_s"].append(kr_s.reshape(DB, T, dr))
            xp = _ffn(xp, g4, wu, wd, a=o_p, wo=bf(w_o_c[j]))
            xs = _ffn(xs, g4, wu, wd, a=o_s.reshape(DB * T, n_heads * dv), wo=bf(w_o_c[j]))
        else:
            n_heads, dh = cache_k_d.shape[3], cache_k_d.shape[4]
            hd = n_heads * dh
            scale = float(dh ** -0.5)
            wdd = (g4, bf(w_in_d[j]), b_f_d[j][None], scale)
            q, k, v, kb, vb, lf, c = _fox_proj(xp, *wdd, n_seq=B)
            n_sub = 128 // dh
            c3 = c.reshape(B, S, n_heads // n_sub, n_sub)
            o_p = _fox_flash(q, kb, vb, c3.transpose(0, 2, 1, 3), c3.transpose(0, 2, 3, 1), n_seq=B, dh=dh)
            outs["k_d_p"].append(k.reshape(B, S, n_heads, dh))
            outs["v_d_p"].append(v.reshape(B, S, n_heads, dh))
            outs["logf_d_p"].append(lf.reshape(B, S, n_heads))
            q_s, k_s, v_s, kb_s, vb_s, lf_s = _fox_proj(xs, *wdd)
            lf_pad = _pad_rows(lf_s.reshape(DB, T, n_heads), BF16_ROWS)
            cpast, cnew = _fox_cpast(cache_logf_d, j, page_table, lf_pad)
            cq_rows = cnew[:, :T, :].reshape(DB, T * n_heads, 1)
            o_s = _fox_decode(q_s.astype(F32).reshape(DB, T, hd), cq_rows,
                              _pad_rows(kb_s.reshape(DB, T, hd), BF16_ROWS),
                              _pad_rows(vb_s.reshape(DB, T, hd), BF16_ROWS), cnew, cpast,
                              cache_k_d.reshape(*cache_k_d.shape[:3], hd),
                              cache_v_d.reshape(*cache_v_d.shape[:3], hd), j, page_table, n_heads)
            outs["k_d_s"].append(k_s.reshape(DB, T, n_heads, dh))
            outs["v_d_s"].append(v_s.reshape(DB, T, n_heads, dh))
            outs["logf_d_s"].append(lf_s.reshape(DB, T, n_heads))
            xp = _ffn(xp, g4, wu, wd, a=o_p, wo=bf(w_o_d[j]))
            xs = _ffn(xs, g4, wu, wd, a=o_s.reshape(DB * T, hd), wo=bf(w_o_d[j]))
    st = lambda name: jnp.stack(outs[name])
    return (xp.reshape(B, S, D), xs.reshape(DB, T, D), st("v_a_s"), st("buf_b_p"), st("buf_b_s"),
            st("ckv_c_p"), st("kr_c_p"), st("ckv_c_s"), st("kr_c_s"),
            st("k_d_p"), st("v_d_p"), st("logf_d_p"), st("k_d_s"), st("v_d_s"), st("logf_d_s"))
```

```python
import functools

import jax
import jax.numpy as jnp
from jax import lax
from jax.experimental import pallas as pl
from jax.experimental.pallas import tpu as pltpu

F32 = jnp.float32
BF16 = jnp.bfloat16
EPS = 1e-6
ROPE_THETA = 10000.0
POOL_WINDOWS = (2, 4, 8, 16)
POOL_HIST = max(POOL_WINDOWS) - 1
HIST_ROWS = 16
BF16_ROWS = 16
V7X_VMEM_LIMIT = 56 * 1024 * 1024
FOX_PAGES_PER_STEP = 8
MLA_PAGES_PER_STEP = 16


def _cparams(*sem):
    return pltpu.CompilerParams(dimension_semantics=sem, vmem_limit_bytes=V7X_VMEM_LIMIT)


def _const_spec(shape):
    nd = len(shape)
    return pl.BlockSpec(shape, lambda *_: (0,) * nd, pipeline_mode=pl.Buffered(1))


def _rms(x, g):
    return x * lax.rsqrt(jnp.mean(x * x, axis=-1, keepdims=True) + EPS) * g


def _dot(a, b):
    return jnp.dot(a, b, preferred_element_type=F32)


def _dot_nt(a, b):
    return lax.dot_general(a, b, (((1,), (1,)), ((), ())), preferred_element_type=F32)


def _split3(x):
    hi = x.astype(BF16)
    r = x - hi.astype(F32)
    mid = r.astype(BF16)
    lo = (r - mid.astype(F32)).astype(BF16)
    return hi, mid, lo


def _cumsum_rows(x, seg=None):
    n = x.shape[0]
    r = lax.broadcasted_iota(jnp.int32, (n, n), 0)
    c = lax.broadcasted_iota(jnp.int32, (n, n), 1)
    keep = c <= r
    if seg is not None and seg < n:
        keep = keep & (c // seg == r // seg)
    tri = jnp.where(keep, 1.0, 0.0).astype(BF16)
    hi, mid, lo = _split3(x)
    return _dot(tri, hi) + _dot(tri, mid) + _dot(tri, lo)


def _online_softmax_cols(st, vt, m_prev, l_prev):
    m_new = jnp.maximum(m_prev, jnp.max(st, axis=0, keepdims=True))
    alpha = jnp.exp(m_prev - m_new)
    p = jnp.exp(st - m_new)
    l_new = alpha * l_prev + jnp.sum(p, axis=0, keepdims=True)
    return m_new, l_new, alpha, _dot(vt, p.astype(BF16))


def _online_softmax_step(s, v, m_ref, l_ref, acc_ref):
    m_prev = m_ref[...]
    m_new = jnp.maximum(m_prev, jnp.max(s, axis=1, keepdims=True))
    alpha = jnp.exp(m_prev - m_new)
    p = jnp.exp(s - m_new)
    l_ref[...] = alpha * l_ref[...] + jnp.sum(p, axis=1, keepdims=True)
    acc_ref[...] = alpha * acc_ref[...] + _dot(p.astype(BF16), v)
    m_ref[...] = m_new


def _init_softmax_state(m_ref, l_ref, acc_ref):
    m_ref[...] = jnp.full(m_ref.shape, -jnp.inf, F32)
    l_ref[...] = jnp.zeros(l_ref.shape, F32)
    acc_ref[...] = jnp.zeros(acc_ref.shape, F32)


def _ffn_kernel(*refs, fc, with_proj):
    if with_proj:
        a_ref, x_ref, wo_ref, g_ref, wu_ref, wd_ref, o_ref = refs
        x1 = x_ref[...] + _rms(_dot(a_ref[...], wo_ref[...]), g_ref[1:2, :])
    else:
        x_ref, g_ref, wu_ref, wd_ref, o_ref = refs
        x1 = x_ref[...]
    h = _rms(x1, g_ref[2:3, :]).astype(BF16)
    acc = None
    for c in range(wu_ref.shape[1] // fc):
        u = jnp.maximum(_dot(h, wu_ref[:, c * fc:(c + 1) * fc]), 0.0)
        d = _dot((u * u).astype(BF16), wd_ref[c * fc:(c + 1) * fc, :])
        acc = d if acc is None else acc + d
    o_ref[...] = x1 + _rms(acc, g_ref[3:4, :])


def _ffn(x, g4, wu, wd, a=None, wo=None, tm=512):
    n, d = x.shape
    tm = min(tm, n)
    assert n % tm == 0
    f = wu.shape[1]
    fc = min(1024, f)
    assert f % fc == 0
    row = lambda i: (i, 0)
    specs, args = [], []
    if a is not None:
        specs += [pl.BlockSpec((tm, a.shape[1]), row), pl.BlockSpec((tm, d), row), _const_spec(wo.shape)]
        args += [a, x, wo]
    else:
        specs += [pl.BlockSpec((tm, d), row)]
        args += [x]
    specs += [_const_spec(g4.shape), _const_spec(wu.shape), _const_spec(wd.shape)]
    args += [g4, wu, wd]
    return pl.pallas_call(
        functools.partial(_ffn_kernel, fc=fc, with_proj=a is not None), name="ffn",
        grid=(n // tm,), in_specs=specs, out_specs=pl.BlockSpec((tm, d), row),
        out_shape=jax.ShapeDtypeStruct((n, d), F32), compiler_params=_cparams("parallel"))(*args)


def _gmlp_kernel(x_ref, g_ref, win_ref, gv_ref, bv_ref, ws_ref, bias_ref, wout_ref, o_ref, v_ref, *, chunk):
    x = x_ref[...]
    d_a = wout_ref.shape[0]
    n_groups = ws_ref.shape[0]
    c_a = d_a // n_groups
    uv = jax.nn.gelu(_dot(_rms(x, g_ref[0:1, :]).astype(BF16), win_ref[...]))
    u = uv[:, :d_a]
    v = uv[:, d_a:]
    vc = v - jnp.mean(v, axis=-1, keepdims=True)
    v = vc * lax.rsqrt(jnp.mean(vc * vc, axis=-1, keepdims=True) + EPS) * gv_ref[...] + bv_ref[...]
    v_ref[...] = v
    vb = v.astype(BF16)
    rows = []
    for c in range(x.shape[0] // chunk):
        cols = [_dot(ws_ref[g], vb[c * chunk:(c + 1) * chunk, g * c_a:(g + 1) * c_a]) for g in range(n_groups)]
        rows.append(jnp.concatenate(cols, axis=1))
    mixed = jnp.concatenate(rows, axis=0) + bias_ref[...]
    m = _dot((u * mixed).astype(BF16), wout_ref[...])
    o_ref[...] = x + _rms(m, g_ref[1:2, :])


def _gmlp(x, g4, w_in, g_v, b_v, ws_eff, bias_rows, w_out, chunk, tm=256):
    n, d = x.shape
    tm = min(tm, n)
    assert n % tm == 0 and tm % chunk == 0
    d_a = w_out.shape[0]
    bias = jnp.tile(bias_rows, (tm // chunk, 1))
    row = lambda i: (i, 0)
    return pl.pallas_call(
        functools.partial(_gmlp_kernel, chunk=chunk), name="gmlp",
        grid=(n // tm,),
        in_specs=[pl.BlockSpec((tm, d), row), _const_spec(g4.shape), _const_spec(w_in.shape),
                  _const_spec(g_v.shape), _const_spec(b_v.shape), _const_spec(ws_eff.shape),
                  _const_spec(bias.shape), _const_spec(w_out.shape)],
        out_specs=[pl.BlockSpec((tm, d), row), pl.BlockSpec((tm, d_a), row)],
        out_shape=[jax.ShapeDtypeStruct((n, d), F32), jax.ShapeDtypeStruct((n, d_a), F32)],
        compiler_params=_cparams("parallel"))(x, g4, w_in, g_v, b_v, ws_eff, bias, w_out)


def _pool_finish(x, dmat, g_ref, wg_ref, sc_ref):
    n_g = wg_ref.shape[0]
    cb = x.shape[1] // n_g
    ys = [_dot(dmat[:, gi * cb:(gi + 1) * cb], wg_ref[gi]) for gi in range(n_g)]
    y = jnp.concatenate(ys, axis=1) * sc_ref[...]
    return x + _rms(y, g_ref[1:2, :])


def _pool_prompt_kernel(x_ref, g_ref, win_ref, wg_ref, sc_ref, o_ref, buf_ref, zbuf, *, tiles_per_seq):
    tm, d = x_ref.shape
    li = pl.program_id(0) % tiles_per_seq

    @pl.when(li == 0)
    def _():
        zbuf[0:HIST_ROWS, :] = jnp.zeros((HIST_ROWS, d), F32)

    x = x_ref[...]
    z = _dot(_rms(x, g_ref[0:1, :]).astype(BF16), win_ref[...])
    zbuf[HIST_ROWS:HIST_ROWS + tm, :] = z
    pos = li * tm + lax.broadcasted_iota(jnp.int32, (tm, 1), 0)
    cb = d // len(POOL_WINDOWS)
    cols = []
    for gi, w in enumerate(POOL_WINDOWS):
        sl = slice(gi * cb, (gi + 1) * cb)
        zg = z[:, sl]
        acc = zg
        for k in range(1, w):
            acc = acc + zbuf[HIST_ROWS - k:HIST_ROWS - k + tm, sl]
        cnt = jnp.minimum(w, pos + 1).astype(F32)
        cols.append(acc / cnt - zg)
    dmat = jnp.concatenate(cols, axis=1).astype(BF16)
    o_ref[...] = _pool_finish(x, dmat, g_ref, wg_ref, sc_ref)
    tail = z[tm - HIST_ROWS:, :]
    zbuf[0:HIST_ROWS, :] = tail
    buf_ref[0] = tail


def _pool_prompt(x, g4, w_in, w_grp, scale, n_seq, tm=256):
    n, d = x.shape
    s = n // n_seq
    tm = min(tm, s)
    assert s % tm == 0 and tm >= HIST_ROWS
    tps = s // tm
    row = lambda i: (i, 0)
    return pl.pallas_call(
        functools.partial(_pool_prompt_kernel, tiles_per_seq=tps), name="pool_prompt",
        grid=(n // tm,),
        in_specs=[pl.BlockSpec((tm, d), row), _const_spec(g4.shape), _const_spec(w_in.shape),
                  _const_spec(w_grp.shape), _const_spec(scale.shape)],
        out_specs=[pl.BlockSpec((tm, d), row), pl.BlockSpec((1, HIST_ROWS, d), lambda i: (i // tps, 0, 0))],
        out_shape=[jax.ShapeDtypeStruct((n, d), F32), jax.ShapeDtypeStruct((n_seq, HIST_ROWS, d), F32)],
        scratch_shapes=[pltpu.VMEM((HIST_ROWS + tm, d), F32)],
        compiler_params=_cparams("arbitrary"))(x, g4, w_in, w_grp, scale)


def _pool_sample_kernel(x_ref, hist_ref, g_ref, win_ref, wg_ref, sc_ref, o_ref, z_ref, *, db, t_new, pos0):
    x = x_ref[...]
    d = x.shape[1]
    z = _dot(_rms(x, g_ref[0:1, :]).astype(BF16), win_ref[...])
    z_ref[...] = z
    cb = d // len(POOL_WINDOWS)

    def rows_at(p, sl):
        if p >= POOL_HIST:
            return z[(p - POOL_HIST) * db:(p - POOL_HIST + 1) * db, sl]
        return hist_ref[p * db:(p + 1) * db, sl]

    drows = []
    for t in range(t_new):
        cols = []
        for gi, w in enumerate(POOL_WINDOWS):
            sl = slice(gi * cb, (gi + 1) * cb)
            acc = rows_at(POOL_HIST + t, sl)
            for k in range(1, w):
                acc = acc + rows_at(POOL_HIST + t - k, sl)
            cnt = float(min(w, pos0 + t + 1))
            cols.append(acc / cnt - z[t * db:(t + 1) * db, sl])
        drows.append(jnp.concatenate(cols, axis=1))
    dmat = jnp.concatenate(drows, axis=0).astype(BF16)
    o_ref[...] = _pool_finish(x, dmat, g_ref, wg_ref, sc_ref)


def _pool_sample(x_tm, hist_tm, g4, w_in, w_grp, scale, db, t_new, pos0):
    n, d = x_tm.shape
    args = (x_tm, hist_tm, g4, w_in, w_grp, scale)
    return pl.pallas_call(
        functools.partial(_pool_sample_kernel, db=db, t_new=t_new, pos0=pos0), name="pool_sample",
        grid=(1,),
        in_specs=[_const_spec(a.shape) for a in args],
        out_specs=[pl.BlockSpec((n, d), lambda i: (0, 0)), pl.BlockSpec((n, d), lambda i: (0, 0))],
        out_shape=[jax.ShapeDtypeStruct((n, d), F32), jax.ShapeDtypeStruct((n, d), F32)],
        compiler_params=_cparams("arbitrary"))(*args)


def _mla_proj_kernel(x_ref, g_ref, win_ref, gq_ref, gkv_ref, wuq_ref, wukt_ref, cos_ref, sin_ref,
                     q_ref, kcat_ref, ckvt_ref, ckv_ref, kr_ref, *, q_rank, kv_rank, scale):
    n_heads, dn, _ = wukt_ref.shape
    x = x_ref[...]
    c = _dot(_rms(x, g_ref[0:1, :]).astype(BF16), win_ref[...])
    c_q = c[:, :q_rank]
    c_kv = c[:, q_rank:q_rank + kv_rank]
    k_r = c[:, q_rank + kv_rank:]
    q = _dot(_rms(c_q, gq_ref[...]).astype(BF16), wuq_ref[...])
    hd = n_heads * dn
    half = (q.shape[1] - hd) // 2
    hr = half // n_heads
    x1 = q[:, hd:hd + half]
    x2 = q[:, hd + half:]
    cosq = cos_ref[...]
    sinq = sin_ref[...]
    r1 = ((x1 * cosq - x2 * sinq) * scale).astype(BF16)
    r2 = ((x1 * sinq + x2 * cosq) * scale).astype(BF16)
    ckv = _rms(c_kv, gkv_ref[...])
    ckv_ref[...] = ckv
    ckvt_ref[...] = ckv.T.astype(BF16)
    k1 = k_r[:, :hr]
    k2 = k_r[:, hr:]
    c1 = cosq[:, :hr]
    s1 = sinq[:, :hr]
    kr = jnp.concatenate([k1 * c1 - k2 * s1, k1 * s1 + k2 * c1], axis=1)
    kr_ref[...] = kr
    kcat_ref[:, :kv_rank] = ckv.astype(BF16)
    kcat_ref[:, kv_rank:] = kr.astype(BF16)
    for h in range(n_heads):
        ql = _dot(q[:, h * dn:(h + 1) * dn].astype(BF16), wukt_ref[h]) * scale
        q_ref[h, :, :kv_rank] = ql.astype(BF16)
        q_ref[h, :, kv_rank:kv_rank + hr] = r1[:, h * hr:(h + 1) * hr]
        q_ref[h, :, kv_rank + hr:] = r2[:, h * hr:(h + 1) * hr]


def _mla_proj(x, g4, w_in, g_q, g_kv, wuq_perm, wukt, cosq, sinq, scale, tm=256):
    n, d = x.shape
    tm = min(tm, n)
    assert n % tm == 0
    n_heads, _, kv_rank = wukt.shape
    q_rank = g_q.shape[1]
    dr = w_in.shape[1] - q_rank - kv_rank
    dk = kv_rank + dr
    row = lambda i: (i, 0)
    return pl.pallas_call(
        functools.partial(_mla_proj_kernel, q_rank=q_rank, kv_rank=kv_rank, scale=scale), name="mla_proj",
        grid=(n // tm,),
        in_specs=[pl.BlockSpec((tm, d), row), _const_spec(g4.shape), _const_spec(w_in.shape),
                  _const_spec(g_q.shape), _const_spec(g_kv.shape), _const_spec(wuq_perm.shape),
                  _const_spec(wukt.shape), pl.BlockSpec((tm, cosq.shape[1]), row),
                  pl.BlockSpec((tm, sinq.shape[1]), row)],
        out_specs=[pl.BlockSpec((n_heads, tm, dk), lambda i: (0, i, 0)), pl.BlockSpec((tm, dk), row),
                   pl.BlockSpec((kv_rank, tm), lambda i: (0, i)),
                   pl.BlockSpec((tm, kv_rank), row), pl.BlockSpec((tm, dr), row)],
        out_shape=[jax.ShapeDtypeStruct((n_heads, n, dk), BF16), jax.ShapeDtypeStruct((n, dk), BF16),
                   jax.ShapeDtypeStruct((kv_rank, n), BF16),
                   jax.ShapeDtypeStruct((n, kv_rank), F32), jax.ShapeDtypeStruct((n, dr), F32)],
        compiler_params=_cparams("parallel"))(x, g4, w_in, g_q, g_kv, wuq_perm, wukt, cosq, sinq)


def _mla_flash_kernel(q_ref, k_ref, vt_ref, wuvt_ref, o_ref, m_ref, l_ref, acc_ref, *, tq, tk):
    i = pl.program_id(1)
    j = pl.program_id(2)
    n_heads = q_ref.shape[0]
    m_cols = n_heads * tq

    @pl.when(j == 0)
    def _():
        _init_softmax_state(m_ref, l_ref, acc_ref)

    def step(masked):
        q = q_ref[...].reshape(m_cols, q_ref.shape[2])
        st = _dot_nt(k_ref[...], q)
        if masked:
            q_pos = i * tq + (lax.broadcasted_iota(jnp.int32, (1, m_cols), 1) & (tq - 1))
            k_pos = j * tk + lax.broadcasted_iota(jnp.int32, (tk, 1), 0)
            st = jnp.where(k_pos <= q_pos, st, -jnp.inf)
        m_new, l_new, alpha, pv = _online_softmax_cols(st, vt_ref[...], m_ref[...], l_ref[...])
        m_ref[...] = m_new
        l_ref[...] = l_new
        acc_ref[...] = alpha * acc_ref[...] + pv

    active = j * tk <= i * tq + tq - 1
    crosses_diagonal = j * tk + tk - 1 > i * tq

    @pl.when(active & crosses_diagonal)
    def _():
        step(True)

    @pl.when(active & jnp.logical_not(crosses_diagonal))
    def _():
        step(False)

    @pl.when(j == pl.num_programs(2) - 1)
    def _():
        ot = (acc_ref[...] / l_ref[...]).astype(BF16)
        outs = [_dot(wuvt_ref[h], ot[:, h * tq:(h + 1) * tq]) for h in range(n_heads)]
        o_ref[...] = jnp.concatenate(outs, axis=0).T.astype(o_ref.dtype)


def _mla_flash(q_cat, kcat, ckvt, wuvt, n_seq, tq=128, tk=512):
    n_heads, n, dk = q_cat.shape
    s = n // n_seq
    tq = min(tq, s)
    tk = min(tk, s)
    assert s % tq == 0 and s % tk == 0 and tq & (tq - 1) == 0
    nq, nk = s // tq, s // tk
    dv, kv_rank = wuvt.shape[1], wuvt.shape[2]
    last_k = lambda i: (i * tq + tq - 1) // tk
    return pl.pallas_call(
        functools.partial(_mla_flash_kernel, tq=tq, tk=tk), name="mla_flash",
        grid=(n_seq, nq, nk),
        in_specs=[pl.BlockSpec((n_heads, tq, dk), lambda b, i, j: (0, b * nq + i, 0)),
                  pl.BlockSpec((tk, dk), lambda b, i, j: (b * nk + jnp.minimum(j, last_k(i)), 0)),
                  pl.BlockSpec((kv_rank, tk), lambda b, i, j: (0, b * nk + jnp.minimum(j, last_k(i)))),
                  _const_spec(wuvt.shape)],
        out_specs=pl.BlockSpec((tq, n_heads * dv), lambda b, i, j: (b * nq + i, 0)),
        out_shape=jax.ShapeDtypeStruct((n, n_heads * dv), BF16),
        scratch_shapes=[pltpu.VMEM((1, n_heads * tq), F32), pltpu.VMEM((1, n_heads * tq), F32),
                        pltpu.VMEM((kv_rank, n_heads * tq), F32)],
        compiler_params=_cparams("parallel", "parallel", "arbitrary"))(q_cat, kcat, ckvt, wuvt)


def _head_diag(full, n_heads, t_new):
    width = full.shape[1]
    dv = width // n_heads
    own = (lax.broadcasted_iota(jnp.int32, (n_heads, width), 1) // dv
           == lax.broadcasted_iota(jnp.int32, (n_heads, width), 0))
    full3 = full.reshape(t_new, n_heads, width)
    return jnp.sum(jnp.where(own[None], full3, 0.0), axis=1)


def _mla_decode_kernel(pt_ref, q_ref, knew_ref, wuv_ref, *rest, n_pp, page, kv_rank, t_new, n_heads):
    ckv_refs = rest[:n_pp]
    krt_refs = rest[n_pp:2 * n_pp]
    o_ref = rest[2 * n_pp]
    m_ref, l_ref, acc_ref, cbuf, rbuf = rest[2 * n_pp + 1:]
    j = pl.program_id(1)

    @pl.when(j == 0)
    def _():
        _init_softmax_state(m_ref, l_ref, acc_ref)

    for r in range(n_pp):
        cbuf[r * page:(r + 1) * page, :] = ckv_refs[r][0, 0].astype(BF16)
        rbuf[:, r * page:(r + 1) * page] = krt_refs[r][0, 0].astype(BF16)
    q = q_ref[...]
    cb = cbuf[...]
    s = _dot_nt(q[:, :kv_rank], cb) + _dot(q[:, kv_rank:], rbuf[...])
    _online_softmax_step(s, cb, m_ref, l_ref, acc_ref)

    @pl.when(j == pl.num_programs(1) - 1)
    def _():
        kn = knew_ref[0]
        s = _dot_nt(q, kn)
        q_t = lax.broadcasted_iota(jnp.int32, (s.shape[0], 1), 0) // n_heads
        k_t = lax.broadcasted_iota(jnp.int32, (1, s.shape[1]), 1)
        s = jnp.where(k_t <= q_t, s, -jnp.inf)
        _online_softmax_step(s, kn[:, :kv_rank], m_ref, l_ref, acc_ref)
        ol = (acc_ref[...] / l_ref[...]).astype(BF16)
        o_ref[0] = _head_diag(_dot(ol, wuv_ref[...]), n_heads, t_new).astype(o_ref.dtype)


def _mla_decode(q_rows, knew, wuv_all, cache_ckv, cache_krt, layer, page_table, t_new, n_heads):
    db, n_pages = page_table.shape
    page, kv_rank = cache_ckv.shape[2], cache_ckv.shape[3]
    dr = cache_krt.shape[2]
    dk = kv_rank + dr
    n_pp = min(MLA_PAGES_PER_STEP, n_pages)
    assert n_pages % n_pp == 0
    rows = t_new * n_heads
    width = wuv_all.shape[1]

    def page_spec(r, shape):
        return pl.BlockSpec((1, 1) + shape, lambda b, j, pt: (layer, pt[b, j * n_pp + r], 0, 0))

    in_specs = ([pl.BlockSpec((rows, dk), lambda b, j, pt: (b, 0)),
                 pl.BlockSpec((1, BF16_ROWS, dk), lambda b, j, pt: (b, 0, 0)),
                 _const_spec(wuv_all.shape)]
                + [page_spec(r, (page, kv_rank)) for r in range(n_pp)]
                + [page_spec(r, (dr, page)) for r in range(n_pp)])
    grid_spec = pltpu.PrefetchScalarGridSpec(
        num_scalar_prefetch=1, grid=(db, n_pages // n_pp), in_specs=in_specs,
        out_specs=pl.BlockSpec((1, t_new, width), lambda b, j, pt: (b, 0, 0)),
        scratch_shapes=[pltpu.VMEM((rows, 1), F32), pltpu.VMEM((rows, 1), F32), pltpu.VMEM((rows, kv_rank), F32),
                        pltpu.VMEM((n_pp * page, kv_rank), BF16), pltpu.VMEM((dr, n_pp * page), BF16)])
    return pl.pallas_call(
        functools.partial(_mla_decode_kernel, n_pp=n_pp, page=page, kv_rank=kv_rank, t_new=t_new, n_heads=n_heads),
        name="mla_decode",
        grid_spec=grid_spec, out_shape=jax.ShapeDtypeStruct((db, t_new, width), BF16),
        compiler_params=_cparams("parallel", "arbitrary"))(
            page_table, q_rows, knew, wuv_all, *([cache_ckv] * n_pp), *([cache_krt] * n_pp))


def _fox_proj_kernel(x_ref, g_ref, win_ref, bf_ref, q_ref, lf_ref, c_ref, *rest, hd, scale, seg, tiles_per_seq):
    x = x_ref[...]
    y = _dot(_rms(x, g_ref[0:1, :]).astype(BF16), win_ref[...])
    q_ref[...] = (y[:, :hd] * scale).astype(BF16)
    k = y[:, hd:2 * hd]
    v = y[:, 2 * hd:3 * hd]
    lf = jax.nn.log_sigmoid(y[:, 3 * hd:] + bf_ref[...])
    lf_ref[...] = lf
    c = _cumsum_rows(lf, seg)
    if tiles_per_seq is None:
        k_ref, v_ref, kb_ref, vb_ref = rest
        k_ref[...] = k
        v_ref[...] = v
        kb_ref[...] = k.astype(BF16)
        vb_ref[...] = v.astype(BF16)
    else:
        kt_ref, vt_ref, kb_ref, vtb_ref, carry = rest

        @pl.when(pl.program_id(0) % tiles_per_seq == 0)
        def _():
            carry[...] = jnp.zeros(carry.shape, F32)

        c = c + carry[...]
        carry[...] = c[c.shape[0] - 1:, :]
        vt = v.T
        kt_ref[0] = k.T
        vt_ref[0] = vt
        kb_ref[...] = k.astype(BF16)
        vtb_ref[0] = vt.astype(BF16)
    c_ref[...] = c


def _fox_proj(x, g4, w_in, b_f, scale, n_seq=None, seg=None, tm=256):
    n, d = x.shape
    n_heads = b_f.shape[1]
    hd = (w_in.shape[1] - n_heads) // 3
    s = n // n_seq if n_seq else n
    tm = min(tm, s)
    assert s % tm == 0 and (n_seq or tm % seg == 0)
    tps = s // tm
    row = lambda i: (i, 0)
    wide = pl.BlockSpec((tm, hd), row)
    narrow = pl.BlockSpec((tm, n_heads), row)
    out_specs = [wide, narrow, narrow]
    out_shape = [jax.ShapeDtypeStruct((n, hd), BF16), jax.ShapeDtypeStruct((n, n_heads), F32),
                 jax.ShapeDtypeStruct((n, n_heads), F32)]
    scratch = []
    if n_seq:
        tspec = pl.BlockSpec((1, hd, tm), lambda i: (i // tps, 0, i % tps))
        out_specs += [tspec, tspec, wide, tspec]
        out_shape += [jax.ShapeDtypeStruct((n_seq, hd, s), F32), jax.ShapeDtypeStruct((n_seq, hd, s), F32),
                      jax.ShapeDtypeStruct((n, hd), BF16), jax.ShapeDtypeStruct((n_seq, hd, s), BF16)]
        scratch.append(pltpu.VMEM((1, n_heads), F32))
    else:
        out_specs += [wide, wide, wide, wide]
        out_shape += [jax.ShapeDtypeStruct((n, hd), F32), jax.ShapeDtypeStruct((n, hd), F32),
                      jax.ShapeDtypeStruct((n, hd), BF16), jax.ShapeDtypeStruct((n, hd), BF16)]
    return pl.pallas_call(
        functools.partial(_fox_proj_kernel, hd=hd, scale=scale, seg=seg, tiles_per_seq=tps if n_seq else None),
        name="fox_proj",
        grid=(n // tm,),
        in_specs=[pl.BlockSpec((tm, d), row), _const_spec(g4.shape), _const_spec(w_in.shape),
                  _const_spec(b_f.shape)],
        out_specs=out_specs, out_shape=out_shape, scratch_shapes=scratch,
        compiler_params=_cparams("arbitrary"))(x, g4, w_in, b_f)


def _fox_flash_kernel(q_ref, k_ref, vt_ref, cq_ref, ck_ref, o_ref, m_ref, l_ref, acc_ref, *, tq, tk, dh):
    i = pl.program_id(2)
    j = pl.program_id(3)
    width = q_ref.shape[1]
    n_sub = width // dh
    lane_head = lax.broadcasted_iota(jnp.int32, (1, width), 1) // dh
    row_head = lax.broadcasted_iota(jnp.int32, (width, 1), 0) // dh

    @pl.when(j == 0)
    def _():
        _init_softmax_state(m_ref, l_ref, acc_ref)

    def step(masked):
        q = q_ref[...]
        k = k_ref[...]
        vt = vt_ref[0]
        cq = cq_ref[0, 0]
        ck = ck_ref[0, 0]
        if masked:
            causal = (j * tk + lax.broadcasted_iota(jnp.int32, (tk, 1), 0)
                      <= i * tq + lax.broadcasted_iota(jnp.int32, (1, tq), 1))
        alpha_full = None
        pv_full = None
        for hh in range(n_sub):
            km = jnp.where(lane_head == hh, k, jnp.zeros_like(k))
            st = _dot_nt(km, q) + (cq[hh:hh + 1, :] - ck[:, hh:hh + 1])
            if masked:
                st = jnp.where(causal, st, -jnp.inf)
            m_new, l_new, alpha, pv = _online_softmax_cols(st, vt, m_ref[hh], l_ref[hh])
            m_ref[hh] = m_new
            l_ref[hh] = l_new
            alpha_b = jnp.broadcast_to(alpha, (width, tq))
            alpha_full = alpha_b if alpha_full is None else jnp.where(row_head == hh, alpha_b, alpha_full)
            pv_full = pv if pv_full is None else jnp.where(row_head == hh, pv, pv_full)
        acc_ref[...] = alpha_full * acc_ref[...] + pv_full

    active = j * tk <= i * tq + tq - 1
    crosses_diagonal = j * tk + tk - 1 > i * tq

    @pl.when(active & crosses_diagonal)
    def _():
        step(True)

    @pl.when(active & jnp.logical_not(crosses_diagonal))
    def _():
        step(False)

    @pl.when(j == pl.num_programs(3) - 1)
    def _():
        l_full = None
        for hh in range(n_sub):
            l_b = jnp.broadcast_to(l_ref[hh], (width, tq))
            l_full = l_b if l_full is None else jnp.where(row_head == hh, l_b, l_full)
        o_ref[...] = (acc_ref[...] / l_full).T.astype(o_ref.dtype)


def _fox_flash(q, kb, vtb, c_rows, c_cols, n_seq, dh, tq=512, tk=512, lanes=128):
    n, hd = q.shape
    s = n // n_seq
    tq = min(tq, s)
    tk = min(tk, s)
    assert s % tq == 0 and s % tk == 0 and hd % lanes == 0 and lanes % dh == 0
    nq, nk = s // tq, s // tk
    n_sub = lanes // dh
    kj = lambda i, j: jnp.minimum(j, (i * tq + tq - 1) // tk)
    return pl.pallas_call(
        functools.partial(_fox_flash_kernel, tq=tq, tk=tk, dh=dh), name="fox_flash",
        grid=(n_seq, hd // lanes, nq, nk),
        in_specs=[pl.BlockSpec((tq, lanes), lambda b, g, i, j: (b * nq + i, g)),
                  pl.BlockSpec((tk, lanes), lambda b, g, i, j: (b * nk + kj(i, j), g)),
                  pl.BlockSpec((1, lanes, tk), lambda b, g, i, j: (b, g, kj(i, j))),
                  pl.BlockSpec((1, 1, n_sub, tq), lambda b, g, i, j: (b, g, 0, i)),
                  pl.BlockSpec((1, 1, tk, n_sub), lambda b, g, i, j: (b, g, kj(i, j), 0))],
        out_specs=pl.BlockSpec((tq, lanes), lambda b, g, i, j: (b * nq + i, g)),
        out_shape=jax.ShapeDtypeStruct((n, hd), BF16),
        scratch_shapes=[pltpu.VMEM((n_sub, 1, tq), F32), pltpu.VMEM((n_sub, 1, tq), F32),
                        pltpu.VMEM((lanes, tq), F32)],
        compiler_params=_cparams("parallel", "parallel", "parallel", "arbitrary"))(q, kb, vtb, c_rows, c_cols)


def _fox_decode_kernel(pt_ref, q_ref, cs_ref, cst_ref, knew_ref, vnew_ref, suf_ref, *rest,
                       n_pp, page, n_heads, t_new):
    kt_refs = rest[:n_pp]
    vt_refs = rest[n_pp:2 * n_pp]
    lft_refs = rest[2 * n_pp:3 * n_pp]
    o_ref = rest[3 * n_pp]
    m_ref, l_ref, acc_ref, carry, kbuf, vbuf = rest[3 * n_pp + 1:]
    j = pl.program_id(1)
    hd = q_ref.shape[2]
    rows = t_new * n_heads
    dh = hd // n_heads

    @pl.when(j == 0)
    def _():
        _init_softmax_state(m_ref, l_ref, acc_ref)
        carry[...] = jnp.zeros(carry.shape, F32)

    own = (lax.broadcasted_iota(jnp.int32, (n_heads, hd), 1) // dh
           == lax.broadcasted_iota(jnp.int32, (n_heads, hd), 0))
    q = q_ref[0]
    qb = jnp.where(own[None], jnp.broadcast_to(q[:, None, :], (t_new, n_heads, hd)), 0.0)
    qb = qb.reshape(rows, hd).astype(BF16)
    cs = cs_ref[0]

    parts = []
    for r in range(n_pp):
        parts += list(_split3(lft_refs[r][0, 0]))
    sums = _dot(jnp.concatenate(parts, axis=0), suf_ref[...])
    cur = carry[...]
    bias = []
    for r in range(n_pp):
        blk = sums[3 * n_heads * r:3 * n_heads * (r + 1), :]
        blk = blk[:n_heads] + blk[n_heads:2 * n_heads] + blk[2 * n_heads:]
        bias.append(blk[:, :page] + cur)
        cur = cur + blk[:, page:page + 1]
        kbuf[:, r * page:(r + 1) * page] = kt_refs[r][0, 0].reshape(hd, page).astype(BF16)
        vbuf[:, r * page:(r + 1) * page] = vt_refs[r][0, 0].reshape(hd, page).astype(BF16)
    carry[...] = cur
    after = jnp.concatenate(bias, axis=1)
    s = _dot(qb, kbuf[...]) + (cs + jnp.concatenate([after] * t_new, axis=0))
    m_prev = m_ref[...]
    m_new = jnp.maximum(m_prev, jnp.max(s, axis=1, keepdims=True))
    alpha = jnp.exp(m_prev - m_new)
    p = jnp.exp(s - m_new)
    l_ref[...] = alpha * l_ref[...] + jnp.sum(p, axis=1, keepdims=True)
    m_ref[...] = m_new
    acc_ref[...] = alpha * acc_ref[...] + _dot_nt(p.astype(BF16), vbuf[...])

    @pl.when(j == pl.num_programs(1) - 1)
    def _():
        kn = knew_ref[0]
        s2 = _dot_nt(qb, kn) + (cs - jnp.concatenate([cst_ref[0]] * t_new, axis=0))
        q_t = lax.broadcasted_iota(jnp.int32, (rows, 1), 0) // n_heads
        k_t = lax.broadcasted_iota(jnp.int32, (1, s2.shape[1]), 1)
        s2 = jnp.where(k_t <= q_t, s2, -jnp.inf)
        _online_softmax_step(s2, vnew_ref[0], m_ref, l_ref, acc_ref)
        o_ref[0] = _head_diag(acc_ref[...] / l_ref[...], n_heads, t_new).astype(o_ref.dtype)


def _fox_decode(q_s, cs_rows, cs_t, knew, vnew, cache_kt, cache_vt, cache_lft, layer, page_table):
    db, n_pages = page_table.shape
    n_heads, dh, page = cache_kt.shape[2:]
    hd = n_heads * dh
    t_new = q_s.shape[1]
    n_pp = min(FOX_PAGES_PER_STEP, n_pages)
    assert n_pages % n_pp == 0
    rows = t_new * n_heads
    per_b = lambda b, j, pt: (b, 0, 0)
    suf = jnp.concatenate([jnp.tril(jnp.ones((page, page), F32), -1),
                           jnp.ones((page, 1), F32), jnp.zeros((page, page - 1), F32)], axis=1).astype(BF16)

    def page_spec(r, shape):
        nd = len(shape)
        return pl.BlockSpec((1, 1) + shape,
                            lambda b, j, pt: (layer, pt[b, n_pages - 1 - (j * n_pp + r)]) + (0,) * nd)

    in_specs = ([pl.BlockSpec((1, t_new, hd), per_b), pl.BlockSpec((1, rows, 1), per_b),
                 pl.BlockSpec((1, n_heads, BF16_ROWS), per_b),
                 pl.BlockSpec((1, BF16_ROWS, hd), per_b), pl.BlockSpec((1, BF16_ROWS, hd), per_b),
                 _const_spec(suf.shape)]
                + [page_spec(r, (n_heads, dh, page)) for r in range(n_pp)] * 2
                + [page_spec(r, (n_heads, page)) for r in range(n_pp)])
    grid_spec = pltpu.PrefetchScalarGridSpec(
        num_scalar_prefetch=1, grid=(db, n_pages // n_pp), in_specs=in_specs,
        out_specs=pl.BlockSpec((1, t_new, hd), per_b),
        scratch_shapes=[pltpu.VMEM((rows, 1), F32), pltpu.VMEM((rows, 1), F32), pltpu.VMEM((rows, hd), F32),
                        pltpu.VMEM((n_heads, 1), F32), pltpu.VMEM((hd, n_pp * page), BF16),
                        pltpu.VMEM((hd, n_pp * page), BF16)])
    return pl.pallas_call(
        functools.partial(_fox_decode_kernel, n_pp=n_pp, page=page, n_heads=n_heads, t_new=t_new),
        name="fox_decode",
        grid_spec=grid_spec, out_shape=jax.ShapeDtypeStruct((db, t_new, hd), BF16),
        compiler_params=_cparams("parallel", "arbitrary"))(
            page_table, q_s, cs_rows, cs_t, knew, vnew, suf,
            *([cache_kt] * n_pp), *([cache_vt] * n_pp), *([cache_lft] * n_pp))


def _pad_rows(a, rows):
    return jnp.pad(a, ((0, 0), (0, rows - a.shape[1]), (0, 0)))


def _rope_tables(pos, dr, n_heads):
    inv = ROPE_THETA ** (-jnp.arange(0, dr, 2, dtype=F32) / dr)
    ang = pos.astype(F32)[:, None] * inv[None, :]
    return jnp.tile(jnp.cos(ang), (1, n_heads)), jnp.tile(jnp.sin(ang), (1, n_heads))


def kernel(x_prompt, x_sample, state_b_buf, cache_ckv_c, cache_kr_c, cache_k_d, cache_v_d, cache_logf_d, page_table, g_norm, w_up, w_down, w_in_a, g_v_a, b_v_a, w_s_a, b_s_a, w_out_a, w_in_b, w_grp_b, scale_b, w_in_c, g_q_c, g_kv_c, w_uq_c, w_uk_c, w_uv_c, w_o_c, w_in_d, b_f_d, w_o_d):
    B, S, D = x_prompt.shape
    DB, T, _ = x_sample.shape
    depth = g_norm.shape[0]
    n_pages = page_table.shape[1]
    page = cache_ckv_c.shape[2]
    past = n_pages * page
    pos_p = jnp.arange(S, dtype=jnp.int32)
    pos_s = past + jnp.arange(T, dtype=jnp.int32)
    bf = lambda w: w.astype(BF16)

    xp = x_prompt.reshape(B * S, D)
    xs = x_sample.reshape(DB * T, D)
    outs = {k: [] for k in ("v_a_s", "buf_b_p", "buf_b_s", "ckv_c_p", "kr_c_p", "ckv_c_s", "kr_c_s",
                            "k_d_p", "v_d_p", "logf_d_p", "k_d_s", "v_d_s", "logf_d_s")}
    for i in range(depth):
        kind, j = i % 4, i // 4
        g4 = g_norm[i]
        wu, wd = bf(w_up[i]), bf(w_down[i])
        if kind == 0:
            n_groups, chunk = w_s_a.shape[1], w_s_a.shape[2]
            d_a = w_out_a.shape[1]
            c_a = d_a // n_groups
            assert chunk % T == 0
            tril = jnp.tril(jnp.ones((chunk, chunk), bool))
            ws_p = bf(jnp.where(tril[None], w_s_a[j], 0.0))
            bias_p = jnp.repeat(b_s_a[j].T, c_a, axis=1)
            tril_t = jnp.tril(jnp.ones((T, T), bool))
            ws_t = jnp.where(tril_t[None], w_s_a[j][:, :T, :T], 0.0)
            eye = jnp.eye(chunk // T, dtype=F32)
            ws_s = bf(jnp.einsum("ab,gts->gatbs", eye, ws_t).reshape(n_groups, chunk, chunk))
            bias_s = jnp.tile(jnp.repeat(b_s_a[j][:, :T].T, c_a, axis=1), (chunk // T, 1))
            wa = (g4, bf(w_in_a[j]), g_v_a[j][None], b_v_a[j][None])
            xp, _ = _gmlp(xp, *wa, ws_p, bias_p, bf(w_out_a[j]), chunk)
            xs, v_new = _gmlp(xs, *wa, ws_s, bias_s, bf(w_out_a[j]), chunk)
            outs["v_a_s"].append(v_new.reshape(DB, T, d_a))
            xp = _ffn(xp, g4, wu, wd)
            xs = _ffn(xs, g4, wu, wd)
        elif kind == 1:
            wb = (g4, bf(w_in_b[j]), bf(w_grp_b[j]), scale_b[j][None])
            xp, buf_p = _pool_prompt(xp, *wb, n_seq=B)
            outs["buf_b_p"].append(buf_p[:, HIST_ROWS - POOL_HIST:, :])
            xs_tm = xs.reshape(DB, T, D).transpose(1, 0, 2).reshape(T * DB, D)
            hist_tm = state_b_buf[j].transpose(1, 0, 2).reshape(POOL_HIST * DB, D)
            xs_tm, z_tm = _pool_sample(xs_tm, hist_tm, *wb, db=DB, t_new=T, pos0=past)
            xs = xs_tm.reshape(T, DB, D).transpose(1, 0, 2).reshape(DB * T, D)
            z_s = z_tm.reshape(T, DB, D).transpose(1, 0, 2)
            outs["buf_b_s"].append(jnp.concatenate([state_b_buf[j], z_s], axis=1)[:, -POOL_HIST:])
            xp = _ffn(xp, g4, wu, wd)
            xs = _ffn(xs, g4, wu, wd)
        elif kind == 2:
            n_heads, kv_rank, dn = w_uk_c.shape[1:]
            dv = w_uv_c.shape[3]
            dr = cache_kr_c.shape[3]
            q_rank = g_q_c.shape[1]
            scale = float((dn + dr) ** -0.5)
            wuq3 = w_uq_c[j].reshape(q_rank, n_heads, dn + dr)
            wuq_perm = bf(jnp.concatenate([wuq3[:, :, :dn].reshape(q_rank, -1),
                                           wuq3[:, :, dn:dn + dr // 2].reshape(q_rank, -1),
                                           wuq3[:, :, dn + dr // 2:].reshape(q_rank, -1)], axis=1))
            wukt = bf(w_uk_c[j].transpose(0, 2, 1))
            wuvt = bf(w_uv_c[j].transpose(0, 2, 1))
            wuv_all = bf(w_uv_c[j].transpose(1, 0, 2).reshape(kv_rank, n_heads * dv))
            wc = (g4, bf(w_in_c[j]), g_q_c[j][None], g_kv_c[j][None], wuq_perm, wukt)
            cos_p, sin_p = _rope_tables(jnp.tile(pos_p, B), dr, n_heads)
            q_cat, kcat, ckvt, ckv, kr = _mla_proj(xp, *wc, cos_p, sin_p, scale)
            o_p = _mla_flash(q_cat, kcat, ckvt, wuvt, n_seq=B)
            outs["ckv_c_p"].append(ckv.reshape(B, S, kv_rank))
            outs["kr_c_p"].append(kr.reshape(B, S, dr))
            cos_s, sin_s = _rope_tables(jnp.tile(pos_s, DB), dr, n_heads)
            q_cat_s, kcat_s, _, ckv_s, kr_s = _mla_proj(xs, *wc, cos_s, sin_s, scale)
            dk = kv_rank + dr
            q_rows = q_cat_s.reshape(n_heads, DB, T, dk).transpose(1, 2, 0, 3).reshape(DB * T * n_heads, dk)
            knew = _pad_rows(kcat_s.reshape(DB, T, dk), BF16_ROWS)
            o_s = _mla_decode(q_rows, knew, wuv_all, cache_ckv_c, cache_kr_c.transpose(0, 1, 3, 2), j,
                              page_table, T, n_heads)
            outs["ckv_c_s"].append(ckv_s.reshape(DB, T, kv_rank))
            outs["kr_c_s"].append(kr_s.reshape(DB, T, dr))
            xp = _ffn(xp, g4, wu, wd, a=o_p, wo=bf(w_o_c[j]))
            xs = _ffn(xs, g4, wu, wd, a=o_s.reshape(DB * T, n_heads * dv), wo=bf(w_o_c[j]))
        else:
            n_heads, dh = cache_k_d.shape[3], cache_k_d.shape[4]
            hd = n_heads * dh
            scale = float(dh ** -0.5)
            wdd = (g4, bf(w_in_d[j]), b_f_d[j][None], scale)
            q, lf, c, kt, vt, kb, vtb = _fox_proj(xp, *wdd, n_seq=B)
            n_sub = 128 // dh
            c3 = c.reshape(B, S, n_heads // n_sub, n_sub)
            o_p = _fox_flash(q, kb, vtb, c3.transpose(0, 2, 3, 1), c3.transpose(0, 2, 1, 3), n_seq=B, dh=dh)
            outs["k_d_p"].append(kt.reshape(B, n_heads, dh, S).transpose(0, 3, 1, 2))
            outs["v_d_p"].append(vt.reshape(B, n_heads, dh, S).transpose(0, 3, 1, 2))
            outs["logf_d_p"].append(lf.reshape(B, S, n_heads))
            q_s, lf_s, cs, k_s, v_s, kb_s, vb_s = _fox_proj(xs, *wdd, seg=T)
            cs3 = cs.reshape(DB, T, n_heads)
            cs_t = jnp.pad(cs3.transpose(0, 2, 1), ((0, 0), (0, 0), (0, BF16_ROWS - T)))
            o_s = _fox_decode(q_s.astype(F32).reshape(DB, T, hd), cs3.reshape(DB, T * n_heads, 1), cs_t,
                              _pad_rows(kb_s.reshape(DB, T, hd), BF16_ROWS),
                              _pad_rows(vb_s.reshape(DB, T, hd), BF16_ROWS),
                              cache_k_d.transpose(0, 1, 3, 4, 2), cache_v_d.transpose(0, 1, 3, 4, 2),
                              cache_logf_d.transpose(0, 1, 3, 2), j, page_table)
            outs["k_d_s"].append(k_s.reshape(DB, T, n_heads, dh))
            outs["v_d_s"].append(v_s.reshape(DB, T, n_heads, dh))
            outs["logf_d_s"].append(lf_s.reshape(DB, T, n_heads))
            xp = _ffn(xp, g4, wu, wd, a=o_p, wo=bf(w_o_d[j]))
            xs = _ffn(xs, g4, wu, wd, a=o_s.reshape(DB * T, hd), wo=bf(w_o_d[j]))
    st = lambda name: jnp.stack(outs[name])
    return (xp.reshape(B, S, D), xs.reshape(DB, T, D), st("v_a_s"), st("buf_b_p"), st("buf_b_s"),
            st("ckv_c_p"), st("kr_c_p"), st("ckv_c_s"), st("kr_c_s"),
            st("k_d_p"), st("v_d_p"), st("logf_d_p"), st("k_d_s"), st("v_d_s"), st("logf_d_s"))
```

```python
import functools

import jax
import jax.numpy as jnp
import numpy as np
from jax import lax
from jax.experimental import pallas as pl
from jax.experimental.pallas import tpu as pltpu

F32 = jnp.float32
BF16 = jnp.bfloat16
EPS = 1e-6
LOG2E = 1.4426950408889634
ROPE_THETA = 10000.0
POOL_WINDOWS = (2, 4, 8, 16)
POOL_HIST = max(POOL_WINDOWS) - 1
HIST_ROWS = 16
BF16_ROWS = 16
V7X_VMEM_LIMIT = 56 * 1024 * 1024
FOX_PAGES_PER_STEP = 8
MLA_PAGES_PER_STEP = 16


def _cparams(*sem):
    return pltpu.CompilerParams(dimension_semantics=sem, vmem_limit_bytes=V7X_VMEM_LIMIT)


def _const_spec(shape):
    nd = len(shape)
    return pl.BlockSpec(shape, lambda *_: (0,) * nd, pipeline_mode=pl.Buffered(1))


def _rms(x, g):
    return x * lax.rsqrt(jnp.mean(x * x, axis=-1, keepdims=True) + EPS) * g


def _dot(a, b):
    return jnp.dot(a, b, preferred_element_type=F32)


def _dot_nt(a, b):
    return lax.dot_general(a, b, (((1,), (1,)), ((), ())), preferred_element_type=F32)


def _split3(x):
    hi = x.astype(BF16)
    r = x - hi.astype(F32)
    mid = r.astype(BF16)
    lo = (r - mid.astype(F32)).astype(BF16)
    return hi, mid, lo


def _cumsum_rows(x, seg=None):
    n = x.shape[0]
    r = lax.broadcasted_iota(jnp.int32, (n, n), 0)
    c = lax.broadcasted_iota(jnp.int32, (n, n), 1)
    keep = c <= r
    if seg is not None and seg < n:
        keep = keep & (c // seg == r // seg)
    tri = jnp.where(keep, 1.0, 0.0).astype(BF16)
    hi, mid, lo = _split3(x)
    return _dot(tri, hi) + _dot(tri, mid) + _dot(tri, lo)


def _online_softmax_cols(st, vt, m_prev, l_prev):
    m_new = jnp.maximum(m_prev, jnp.max(st, axis=0, keepdims=True))
    alpha = jnp.exp2(m_prev - m_new)
    p = jnp.exp2(st - m_new)
    l_new = alpha * l_prev + jnp.sum(p, axis=0, keepdims=True)
    return m_new, l_new, alpha, _dot(vt, p.astype(BF16))


def _online_softmax_step(s, v, m_ref, l_ref, acc_ref):
    m_prev = m_ref[...]
    m_new = jnp.maximum(m_prev, jnp.max(s, axis=1, keepdims=True))
    alpha = jnp.exp2(m_prev - m_new)
    p = jnp.exp2(s - m_new)
    l_ref[...] = alpha * l_ref[...] + jnp.sum(p, axis=1, keepdims=True)
    acc_ref[...] = alpha * acc_ref[...] + _dot(p.astype(BF16), v)
    m_ref[...] = m_new


def _init_softmax_state(m_ref, l_ref, acc_ref):
    m_ref[...] = jnp.full(m_ref.shape, -jnp.inf, F32)
    l_ref[...] = jnp.zeros(l_ref.shape, F32)
    acc_ref[...] = jnp.zeros(acc_ref.shape, F32)


def _ffn_kernel(*refs, fc, with_proj):
    if with_proj:
        a_ref, x_ref, wo_ref, g_ref, wu_ref, wd_ref, o_ref = refs
        x1 = x_ref[...] + _rms(_dot(a_ref[...], wo_ref[...]), g_ref[1:2, :])
    else:
        x_ref, g_ref, wu_ref, wd_ref, o_ref = refs
        x1 = x_ref[...]
    h = _rms(x1, g_ref[2:3, :]).astype(BF16)
    acc = None
    for c in range(wu_ref.shape[1] // fc):
        u = jnp.maximum(_dot(h, wu_ref[:, c * fc:(c + 1) * fc]), 0.0)
        d = _dot((u * u).astype(BF16), wd_ref[c * fc:(c + 1) * fc, :])
        acc = d if acc is None else acc + d
    o_ref[...] = x1 + _rms(acc, g_ref[3:4, :])


def _ffn(x, g4, wu, wd, a=None, wo=None, tm=512):
    n, d = x.shape
    tm = min(tm, n)
    assert n % tm == 0
    f = wu.shape[1]
    fc = min(1024, f)
    assert f % fc == 0
    row = lambda i: (i, 0)
    specs, args = [], []
    if a is not None:
        specs += [pl.BlockSpec((tm, a.shape[1]), row), pl.BlockSpec((tm, d), row), _const_spec(wo.shape)]
        args += [a, x, wo]
    else:
        specs += [pl.BlockSpec((tm, d), row)]
        args += [x]
    specs += [_const_spec(g4.shape), _const_spec(wu.shape), _const_spec(wd.shape)]
    args += [g4, wu, wd]
    return pl.pallas_call(
        functools.partial(_ffn_kernel, fc=fc, with_proj=a is not None), name="ffn",
        grid=(n // tm,), in_specs=specs, out_specs=pl.BlockSpec((tm, d), row),
        out_shape=jax.ShapeDtypeStruct((n, d), F32), compiler_params=_cparams("parallel"))(*args)


def _gmlp_kernel(x_ref, g_ref, win_ref, gv_ref, bv_ref, ws_ref, bias_ref, wout_ref, o_ref, v_ref, *, chunk):
    x = x_ref[...]
    d_a = wout_ref.shape[0]
    n_groups = ws_ref.shape[0]
    c_a = d_a // n_groups
    uv = jax.nn.gelu(_dot(_rms(x, g_ref[0:1, :]).astype(BF16), win_ref[...]))
    u = uv[:, :d_a]
    v = uv[:, d_a:]
    vc = v - jnp.mean(v, axis=-1, keepdims=True)
    v = vc * lax.rsqrt(jnp.mean(vc * vc, axis=-1, keepdims=True) + EPS) * gv_ref[...] + bv_ref[...]
    v_ref[...] = v
    vb = v.astype(BF16)
    rows = []
    for c in range(x.shape[0] // chunk):
        cols = [_dot(ws_ref[g], vb[c * chunk:(c + 1) * chunk, g * c_a:(g + 1) * c_a]) for g in range(n_groups)]
        rows.append(jnp.concatenate(cols, axis=1))
    mixed = jnp.concatenate(rows, axis=0) + bias_ref[...]
    m = _dot((u * mixed).astype(BF16), wout_ref[...])
    o_ref[...] = x + _rms(m, g_ref[1:2, :])


def _gmlp(x, g4, w_in, g_v, b_v, ws_eff, bias_rows, w_out, chunk, tm=256):
    n, d = x.shape
    tm = min(tm, n)
    assert n % tm == 0 and tm % chunk == 0
    d_a = w_out.shape[0]
    bias = jnp.tile(bias_rows, (tm // chunk, 1))
    row = lambda i: (i, 0)
    return pl.pallas_call(
        functools.partial(_gmlp_kernel, chunk=chunk), name="gmlp",
        grid=(n // tm,),
        in_specs=[pl.BlockSpec((tm, d), row), _const_spec(g4.shape), _const_spec(w_in.shape),
                  _const_spec(g_v.shape), _const_spec(b_v.shape), _const_spec(ws_eff.shape),
                  _const_spec(bias.shape), _const_spec(w_out.shape)],
        out_specs=[pl.BlockSpec((tm, d), row), pl.BlockSpec((tm, d_a), row)],
        out_shape=[jax.ShapeDtypeStruct((n, d), F32), jax.ShapeDtypeStruct((n, d_a), F32)],
        compiler_params=_cparams("parallel"))(x, g4, w_in, g_v, b_v, ws_eff, bias, w_out)


def _pool_finish(x, dmat, g_ref, wg_ref, sc_ref):
    n_g = wg_ref.shape[0]
    cb = x.shape[1] // n_g
    ys = [_dot(dmat[:, gi * cb:(gi + 1) * cb], wg_ref[gi]) for gi in range(n_g)]
    y = jnp.concatenate(ys, axis=1) * sc_ref[...]
    return x + _rms(y, g_ref[1:2, :])


def _pool_prompt_kernel(x_ref, g_ref, win_ref, wg_ref, sc_ref, o_ref, buf_ref, zbuf, *, tiles_per_seq):
    tm, d = x_ref.shape
    li = pl.program_id(0) % tiles_per_seq

    @pl.when(li == 0)
    def _():
        zbuf[0:HIST_ROWS, :] = jnp.zeros((HIST_ROWS, d), F32)

    x = x_ref[...]
    z = _dot(_rms(x, g_ref[0:1, :]).astype(BF16), win_ref[...])
    zbuf[HIST_ROWS:HIST_ROWS + tm, :] = z
    pos = li * tm + lax.broadcasted_iota(jnp.int32, (tm, 1), 0)
    cb = d // len(POOL_WINDOWS)
    cols = []
    for gi, w in enumerate(POOL_WINDOWS):
        sl = slice(gi * cb, (gi + 1) * cb)
        zg = z[:, sl]
        acc = zg
        for k in range(1, w):
            acc = acc + zbuf[HIST_ROWS - k:HIST_ROWS - k + tm, sl]
        cnt = jnp.minimum(w, pos + 1).astype(F32)
        cols.append(acc / cnt - zg)
    dmat = jnp.concatenate(cols, axis=1).astype(BF16)
    o_ref[...] = _pool_finish(x, dmat, g_ref, wg_ref, sc_ref)
    tail = z[tm - HIST_ROWS:, :]
    zbuf[0:HIST_ROWS, :] = tail
    buf_ref[0] = tail


def _pool_prompt(x, g4, w_in, w_grp, scale, n_seq, tm=256):
    n, d = x.shape
    s = n // n_seq
    tm = min(tm, s)
    assert s % tm == 0 and tm >= HIST_ROWS
    tps = s // tm
    row = lambda i: (i, 0)
    return pl.pallas_call(
        functools.partial(_pool_prompt_kernel, tiles_per_seq=tps), name="pool_prompt",
        grid=(n // tm,),
        in_specs=[pl.BlockSpec((tm, d), row), _const_spec(g4.shape), _const_spec(w_in.shape),
                  _const_spec(w_grp.shape), _const_spec(scale.shape)],
        out_specs=[pl.BlockSpec((tm, d), row), pl.BlockSpec((1, HIST_ROWS, d), lambda i: (i // tps, 0, 0))],
        out_shape=[jax.ShapeDtypeStruct((n, d), F32), jax.ShapeDtypeStruct((n_seq, HIST_ROWS, d), F32)],
        scratch_shapes=[pltpu.VMEM((HIST_ROWS + tm, d), F32)],
        compiler_params=_cparams("arbitrary"))(x, g4, w_in, w_grp, scale)


def _pool_sample_kernel(x_ref, hist_ref, g_ref, win_ref, wg_ref, sc_ref, o_ref, z_ref, *, db, t_new, pos0):
    x = x_ref[...]
    d = x.shape[1]
    z = _dot(_rms(x, g_ref[0:1, :]).astype(BF16), win_ref[...])
    z_ref[...] = z
    cb = d // len(POOL_WINDOWS)

    def rows_at(p, sl):
        if p >= POOL_HIST:
            return z[(p - POOL_HIST) * db:(p - POOL_HIST + 1) * db, sl]
        return hist_ref[p * db:(p + 1) * db, sl]

    drows = []
    for t in range(t_new):
        cols = []
        for gi, w in enumerate(POOL_WINDOWS):
            sl = slice(gi * cb, (gi + 1) * cb)
            acc = rows_at(POOL_HIST + t, sl)
            for k in range(1, w):
                acc = acc + rows_at(POOL_HIST + t - k, sl)
            cnt = float(min(w, pos0 + t + 1))
            cols.append(acc / cnt - z[t * db:(t + 1) * db, sl])
        drows.append(jnp.concatenate(cols, axis=1))
    dmat = jnp.concatenate(drows, axis=0).astype(BF16)
    o_ref[...] = _pool_finish(x, dmat, g_ref, wg_ref, sc_ref)


def _pool_sample(x_tm, hist_tm, g4, w_in, w_grp, scale, db, t_new, pos0):
    n, d = x_tm.shape
    args = (x_tm, hist_tm, g4, w_in, w_grp, scale)
    return pl.pallas_call(
        functools.partial(_pool_sample_kernel, db=db, t_new=t_new, pos0=pos0), name="pool_sample",
        grid=(1,),
        in_specs=[_const_spec(a.shape) for a in args],
        out_specs=[pl.BlockSpec((n, d), lambda i: (0, 0)), pl.BlockSpec((n, d), lambda i: (0, 0))],
        out_shape=[jax.ShapeDtypeStruct((n, d), F32), jax.ShapeDtypeStruct((n, d), F32)],
        compiler_params=_cparams("arbitrary"))(*args)


def _mla_proj_kernel(x_ref, g_ref, win_ref, gq_ref, gkv_ref, wuq_ref, wukt_ref, cos_ref, sin_ref,
                     q_ref, kcat_ref, ckvt_ref, ckv_ref, kr_ref, *, q_rank, kv_rank, scale):
    n_heads, dn, _ = wukt_ref.shape
    x = x_ref[...]
    c = _dot(_rms(x, g_ref[0:1, :]).astype(BF16), win_ref[...])
    c_q = c[:, :q_rank]
    c_kv = c[:, q_rank:q_rank + kv_rank]
    k_r = c[:, q_rank + kv_rank:]
    q = _dot(_rms(c_q, gq_ref[...]).astype(BF16), wuq_ref[...])
    hd = n_heads * dn
    half = (q.shape[1] - hd) // 2
    hr = half // n_heads
    x1 = q[:, hd:hd + half]
    x2 = q[:, hd + half:]
    cosq = cos_ref[...]
    sinq = sin_ref[...]
    r1 = ((x1 * cosq - x2 * sinq) * scale).astype(BF16)
    r2 = ((x1 * sinq + x2 * cosq) * scale).astype(BF16)
    ckv = _rms(c_kv, gkv_ref[...])
    ckv_ref[...] = ckv
    ckvt_ref[...] = ckv.T.astype(BF16)
    k1 = k_r[:, :hr]
    k2 = k_r[:, hr:]
    c1 = cosq[:, :hr]
    s1 = sinq[:, :hr]
    kr = jnp.concatenate([k1 * c1 - k2 * s1, k1 * s1 + k2 * c1], axis=1)
    kr_ref[...] = kr
    kcat_ref[:, :kv_rank] = ckv.astype(BF16)
    kcat_ref[:, kv_rank:] = kr.astype(BF16)
    for h in range(n_heads):
        ql = _dot(q[:, h * dn:(h + 1) * dn].astype(BF16), wukt_ref[h]) * scale
        q_ref[h, :, :kv_rank] = ql.astype(BF16)
        q_ref[h, :, kv_rank:kv_rank + hr] = r1[:, h * hr:(h + 1) * hr]
        q_ref[h, :, kv_rank + hr:] = r2[:, h * hr:(h + 1) * hr]


def _mla_proj(x, g4, w_in, g_q, g_kv, wuq_perm, wukt, cosq, sinq, scale, tm=256):
    n, d = x.shape
    tm = min(tm, n)
    assert n % tm == 0
    n_heads, _, kv_rank = wukt.shape
    q_rank = g_q.shape[1]
    dr = w_in.shape[1] - q_rank - kv_rank
    dk = kv_rank + dr
    row = lambda i: (i, 0)
    return pl.pallas_call(
        functools.partial(_mla_proj_kernel, q_rank=q_rank, kv_rank=kv_rank, scale=scale), name="mla_proj",
        grid=(n // tm,),
        in_specs=[pl.BlockSpec((tm, d), row), _const_spec(g4.shape), _const_spec(w_in.shape),
                  _const_spec(g_q.shape), _const_spec(g_kv.shape), _const_spec(wuq_perm.shape),
                  _const_spec(wukt.shape), pl.BlockSpec((tm, cosq.shape[1]), row),
                  pl.BlockSpec((tm, sinq.shape[1]), row)],
        out_specs=[pl.BlockSpec((n_heads, tm, dk), lambda i: (0, i, 0)), pl.BlockSpec((tm, dk), row),
                   pl.BlockSpec((kv_rank, tm), lambda i: (0, i)),
                   pl.BlockSpec((tm, kv_rank), row), pl.BlockSpec((tm, dr), row)],
        out_shape=[jax.ShapeDtypeStruct((n_heads, n, dk), BF16), jax.ShapeDtypeStruct((n, dk), BF16),
                   jax.ShapeDtypeStruct((kv_rank, n), BF16),
                   jax.ShapeDtypeStruct((n, kv_rank), F32), jax.ShapeDtypeStruct((n, dr), F32)],
        compiler_params=_cparams("parallel"))(x, g4, w_in, g_q, g_kv, wuq_perm, wukt, cosq, sinq)


def _causal_tile_pairs(s, tq, tk):
    pairs = [(i, j) for i in range(s // tq) for j in range((i * tq + tq - 1) // tk + 1)]
    return (jnp.asarray([p[0] for p in pairs], jnp.int32), jnp.asarray([p[1] for p in pairs], jnp.int32))


def _mla_flash_kernel(qi_ref, kj_ref, q_ref, k_ref, vt_ref, wuvt_ref, o_ref, m_ref, l_ref, acc_ref, *, tq, tk):
    p = pl.program_id(1)
    i = qi_ref[p]
    j = kj_ref[p]
    n_heads = q_ref.shape[0]
    m_cols = n_heads * tq

    @pl.when(j == 0)
    def _():
        _init_softmax_state(m_ref, l_ref, acc_ref)

    def step(masked):
        q = q_ref[...].reshape(m_cols, q_ref.shape[2])
        st = _dot_nt(k_ref[...], q)
        if masked:
            q_pos = i * tq + (lax.broadcasted_iota(jnp.int32, (1, m_cols), 1) & (tq - 1))
            k_pos = j * tk + lax.broadcasted_iota(jnp.int32, (tk, 1), 0)
            st = jnp.where(k_pos <= q_pos, st, -jnp.inf)
        m_new, l_new, alpha, pv = _online_softmax_cols(st, vt_ref[...], m_ref[...], l_ref[...])
        m_ref[...] = m_new
        l_ref[...] = l_new
        acc_ref[...] = alpha * acc_ref[...] + pv

    crosses_diagonal = j * tk + tk - 1 > i * tq

    @pl.when(crosses_diagonal)
    def _():
        step(True)

    @pl.when(jnp.logical_not(crosses_diagonal))
    def _():
        step(False)

    @pl.when((j + 1) * tk > i * tq + tq - 1)
    def _():
        ot = (acc_ref[...] / l_ref[...]).astype(BF16)
        outs = [_dot(wuvt_ref[h], ot[:, h * tq:(h + 1) * tq]) for h in range(n_heads)]
        o_ref[...] = jnp.concatenate(outs, axis=0).T.astype(o_ref.dtype)


def _mla_flash(q_cat, kcat, ckvt, wuvt, n_seq, tq=256, tk=512):
    n_heads, n, dk = q_cat.shape
    s = n // n_seq
    tq = min(tq, s)
    tk = min(tk, s)
    assert s % tq == 0 and s % tk == 0 and tq & (tq - 1) == 0
    nq, nk = s // tq, s // tk
    dv, kv_rank = wuvt.shape[1], wuvt.shape[2]
    qi, kj = _causal_tile_pairs(s, tq, tk)
    grid_spec = pltpu.PrefetchScalarGridSpec(
        num_scalar_prefetch=2, grid=(n_seq, qi.shape[0]),
        in_specs=[pl.BlockSpec((n_heads, tq, dk), lambda b, p, qi, kj: (0, b * nq + qi[p], 0)),
                  pl.BlockSpec((tk, dk), lambda b, p, qi, kj: (b * nk + kj[p], 0)),
                  pl.BlockSpec((kv_rank, tk), lambda b, p, qi, kj: (0, b * nk + kj[p])),
                  _const_spec(wuvt.shape)],
        out_specs=pl.BlockSpec((tq, n_heads * dv), lambda b, p, qi, kj: (b * nq + qi[p], 0)),
        scratch_shapes=[pltpu.VMEM((1, n_heads * tq), F32), pltpu.VMEM((1, n_heads * tq), F32),
                        pltpu.VMEM((kv_rank, n_heads * tq), F32)])
    return pl.pallas_call(
        functools.partial(_mla_flash_kernel, tq=tq, tk=tk), name="mla_flash", grid_spec=grid_spec,
        out_shape=jax.ShapeDtypeStruct((n, n_heads * dv), BF16),
        compiler_params=_cparams("parallel", "arbitrary"))(qi, kj, q_cat, kcat, ckvt, wuvt)


def _head_diag(full, n_heads, t_new):
    width = full.shape[1]
    dv = width // n_heads
    own = (lax.broadcasted_iota(jnp.int32, (n_heads, width), 1) // dv
           == lax.broadcasted_iota(jnp.int32, (n_heads, width), 0))
    full3 = full.reshape(t_new, n_heads, width)
    return jnp.sum(jnp.where(own[None], full3, 0.0), axis=1)


def _mla_decode_kernel(pt_ref, q_ref, knew_ref, wuv_ref, *rest, n_b, n_pp, page, kv_rank, t_new, n_heads):
    n_pg = n_b * n_pp
    ckv_refs = rest[:n_pg]
    krt_refs = rest[n_pg:2 * n_pg]
    o_ref = rest[2 * n_pg]
    scratch = rest[2 * n_pg + 1:]
    state = [scratch[5 * u:5 * u + 5] for u in range(n_b)]
    j = pl.program_id(1)
    rows = t_new * n_heads

    @pl.when(j == 0)
    def _():
        for m_ref, l_ref, acc_ref, _, _ in state:
            _init_softmax_state(m_ref, l_ref, acc_ref)

    for u, (m_ref, l_ref, acc_ref, cbuf, rbuf) in enumerate(state):
        for r in range(n_pp):
            cbuf[r * page:(r + 1) * page, :] = ckv_refs[u * n_pp + r][0, 0].astype(BF16)
            rbuf[:, r * page:(r + 1) * page] = krt_refs[u * n_pp + r][0, 0].astype(BF16)
        q = q_ref[u * rows:(u + 1) * rows, :]
        cb = cbuf[...]
        s = _dot_nt(q[:, :kv_rank], cb) + _dot(q[:, kv_rank:], rbuf[...])
        _online_softmax_step(s, cb, m_ref, l_ref, acc_ref)

    @pl.when(j == pl.num_programs(1) - 1)
    def _():
        for u, (m_ref, l_ref, acc_ref, _, _) in enumerate(state):
            q = q_ref[u * rows:(u + 1) * rows, :]
            kn = knew_ref[u]
            s = _dot_nt(q, kn)
            q_t = lax.broadcasted_iota(jnp.int32, (s.shape[0], 1), 0) // n_heads
            k_t = lax.broadcasted_iota(jnp.int32, (1, s.shape[1]), 1)
            s = jnp.where(k_t <= q_t, s, -jnp.inf)
            _online_softmax_step(s, kn[:, :kv_rank], m_ref, l_ref, acc_ref)
            ol = (acc_ref[...] / l_ref[...]).astype(BF16)
            o_ref[u] = _head_diag(_dot(ol, wuv_ref[...]), n_heads, t_new).astype(o_ref.dtype)


def _mla_decode(q_rows, knew, wuv_all, cache_ckv, cache_krt, layer, page_table, t_new, n_heads, n_b=2):
    db, n_pages = page_table.shape
    page, kv_rank = cache_ckv.shape[2], cache_ckv.shape[3]
    dr = cache_krt.shape[2]
    dk = kv_rank + dr
    n_pp = min(MLA_PAGES_PER_STEP, n_pages)
    assert n_pages % n_pp == 0 and db % n_b == 0
    rows = t_new * n_heads
    width = wuv_all.shape[1]

    def page_spec(u, r, shape):
        return pl.BlockSpec((1, 1) + shape, lambda b, j, pt: (layer, pt[b * n_b + u, j * n_pp + r], 0, 0))

    in_specs = ([pl.BlockSpec((n_b * rows, dk), lambda b, j, pt: (b, 0)),
                 pl.BlockSpec((n_b, BF16_ROWS, dk), lambda b, j, pt: (b, 0, 0)),
                 _const_spec(wuv_all.shape)]
                + [page_spec(u, r, (page, kv_rank)) for u in range(n_b) for r in range(n_pp)]
                + [page_spec(u, r, (dr, page)) for u in range(n_b) for r in range(n_pp)])
    per_sample = [pltpu.VMEM((rows, 1), F32), pltpu.VMEM((rows, 1), F32), pltpu.VMEM((rows, kv_rank), F32),
                  pltpu.VMEM((n_pp * page, kv_rank), BF16), pltpu.VMEM((dr, n_pp * page), BF16)]
    grid_spec = pltpu.PrefetchScalarGridSpec(
        num_scalar_prefetch=1, grid=(db // n_b, n_pages // n_pp), in_specs=in_specs,
        out_specs=pl.BlockSpec((n_b, t_new, width), lambda b, j, pt: (b, 0, 0)),
        scratch_shapes=per_sample * n_b)
    return pl.pallas_call(
        functools.partial(_mla_decode_kernel, n_b=n_b, n_pp=n_pp, page=page, kv_rank=kv_rank, t_new=t_new,
                          n_heads=n_heads),
        name="mla_decode",
        grid_spec=grid_spec, out_shape=jax.ShapeDtypeStruct((db, t_new, width), BF16),
        compiler_params=_cparams("parallel", "arbitrary"))(
            page_table, q_rows, knew, wuv_all, *([cache_ckv] * (n_b * n_pp)), *([cache_krt] * (n_b * n_pp)))


def _fox_proj_kernel(x_ref, g_ref, win_ref, bf_ref, place_ref, ones_ref, q_ref, lf_ref, c_ref, *rest,
                     hd, scale, seg, tiles_per_seq):
    x = x_ref[...]
    y = _dot(_rms(x, g_ref[0:1, :]).astype(BF16), win_ref[...])
    q_ref[...] = (y[:, :hd] * scale).astype(BF16)
    k = y[:, hd:2 * hd]
    v = y[:, 2 * hd:3 * hd]
    lf = jax.nn.log_sigmoid(y[:, 3 * hd:] + bf_ref[...])
    lf_ref[...] = lf
    c = _cumsum_rows(lf, seg)
    if tiles_per_seq is None:
        k_ref, v_ref, kb_ref, vb_ref = rest
        k_ref[...] = k
        v_ref[...] = v
        kb_ref[...] = k.astype(BF16)
        vb_ref[...] = v.astype(BF16)
    else:
        kt_ref, vt_ref, kb_ref, vtb_ref, qa_ref, ka_ref, carry = rest

        @pl.when(pl.program_id(0) % tiles_per_seq == 0)
        def _():
            carry[...] = jnp.zeros(carry.shape, F32)

        c = c + carry[...]
        carry[...] = c[c.shape[0] - 1:, :]
        vt = v.T
        kt_ref[0] = k.T
        vt_ref[0] = vt
        kb_ref[...] = k.astype(BF16)
        vtb_ref[0] = vt.astype(BF16)
        n_heads = c.shape[1]
        parts = _split3(c * LOG2E)
        for side, out_ref in ((0, qa_ref), (1, ka_ref)):
            aug = ones_ref[side:side + 1, :]
            for t, part in enumerate(parts):
                aug = aug + _dot(part, place_ref[side, t * n_heads:(t + 1) * n_heads, :])
            out_ref[...] = aug.astype(BF16)
    c_ref[...] = c


def _fox_proj(x, g4, w_in, b_f, scale, n_seq=None, seg=None, tm=256):
    n, d = x.shape
    n_heads = b_f.shape[1]
    hd = (w_in.shape[1] - n_heads) // 3
    s = n // n_seq if n_seq else n
    tm = min(tm, s)
    assert s % tm == 0 and (n_seq or tm % seg == 0)
    tps = s // tm
    row = lambda i: (i, 0)
    wide = pl.BlockSpec((tm, hd), row)
    narrow = pl.BlockSpec((tm, n_heads), row)
    out_specs = [wide, narrow, narrow]
    out_shape = [jax.ShapeDtypeStruct((n, hd), BF16), jax.ShapeDtypeStruct((n, n_heads), F32),
                 jax.ShapeDtypeStruct((n, n_heads), F32)]
    scratch = []
    if n_seq:
        tspec = pl.BlockSpec((1, hd, tm), lambda i: (i // tps, 0, i % tps))
        out_specs += [tspec, tspec, wide, tspec, wide, wide]
        out_shape += [jax.ShapeDtypeStruct((n_seq, hd, s), F32), jax.ShapeDtypeStruct((n_seq, hd, s), F32),
                      jax.ShapeDtypeStruct((n, hd), BF16), jax.ShapeDtypeStruct((n_seq, hd, s), BF16),
                      jax.ShapeDtypeStruct((n, hd), BF16), jax.ShapeDtypeStruct((n, hd), BF16)]
        scratch.append(pltpu.VMEM((1, n_heads), F32))
    else:
        out_specs += [wide, wide, wide, wide]
        out_shape += [jax.ShapeDtypeStruct((n, hd), F32), jax.ShapeDtypeStruct((n, hd), F32),
                      jax.ShapeDtypeStruct((n, hd), BF16), jax.ShapeDtypeStruct((n, hd), BF16)]
    place, ones = _fox_aug_tables(n_heads, hd // n_heads)
    return pl.pallas_call(
        functools.partial(_fox_proj_kernel, hd=hd, scale=scale, seg=seg, tiles_per_seq=tps if n_seq else None),
        name="fox_proj",
        grid=(n // tm,),
        in_specs=[pl.BlockSpec((tm, d), row), _const_spec(g4.shape), _const_spec(w_in.shape),
                  _const_spec(b_f.shape), _const_spec(place.shape), _const_spec(ones.shape)],
        out_specs=out_specs, out_shape=out_shape, scratch_shapes=scratch,
        compiler_params=_cparams("arbitrary"))(x, g4, w_in, b_f, place, ones)


FOX_LANES = 128
FOX_AUG_STRIDE = 8


def _fox_aug_tables(n_heads, dh):
    n_sub = FOX_LANES // dh
    place = np.zeros((2, 3 * n_heads, n_heads * dh), np.float32)
    ones = np.zeros((2, n_heads * dh), np.float32)
    for head in range(n_heads):
        base = (head // n_sub) * FOX_LANES + (head % n_sub) * FOX_AUG_STRIDE
        for t in range(3):
            place[0, t * n_heads + head, base + 3 + t] = 1.0
            place[1, t * n_heads + head, base + t] = -1.0
            ones[0, base + t] = 1.0
            ones[1, base + 3 + t] = 1.0
    return jnp.asarray(place, BF16), jnp.asarray(ones, F32)


def _fox_flash_kernel(qi_ref, kj_ref, q_ref, qa_ref, k_ref, ka_ref, vt_ref, o_ref, m_ref, l_ref, acc_ref,
                      *, tq, tk, dh):
    p = pl.program_id(2)
    i = qi_ref[p]
    j = kj_ref[p]
    width = q_ref.shape[1]
    n_sub = width // dh
    lane = lax.broadcasted_iota(jnp.int32, (1, 2 * width), 1)

    @pl.when(j == 0)
    def _():
        _init_softmax_state(m_ref, l_ref, acc_ref)

    def step(masked):
        q = jnp.concatenate([q_ref[...], qa_ref[...]], axis=1)
        k = jnp.concatenate([k_ref[...], ka_ref[...]], axis=1)
        if masked:
            causal = (j * tk + lax.broadcasted_iota(jnp.int32, (tk, 1), 0)
                      <= i * tq + lax.broadcasted_iota(jnp.int32, (1, tq), 1))
        for hh in range(n_sub):
            own = ((lane // dh == hh)
                   | ((lane >= width + hh * FOX_AUG_STRIDE) & (lane < width + (hh + 1) * FOX_AUG_STRIDE)))
            st = _dot_nt(jnp.where(own, k, jnp.zeros_like(k)), q)
            if masked:
                st = jnp.where(causal, st, -jnp.inf)
            rows = slice(hh * dh, (hh + 1) * dh)
            m_new, l_new, alpha, pv = _online_softmax_cols(st, vt_ref[0, rows, :], m_ref[hh], l_ref[hh])
            m_ref[hh] = m_new
            l_ref[hh] = l_new
            acc_ref[rows, :] = alpha * acc_ref[rows, :] + pv

    crosses_diagonal = j * tk + tk - 1 > i * tq

    @pl.when(crosses_diagonal)
    def _():
        step(True)

    @pl.when(jnp.logical_not(crosses_diagonal))
    def _():
        step(False)

    @pl.when((j + 1) * tk > i * tq + tq - 1)
    def _():
        outs = [acc_ref[hh * dh:(hh + 1) * dh, :] / l_ref[hh] for hh in range(n_sub)]
        o_ref[...] = jnp.concatenate(outs, axis=0).T.astype(o_ref.dtype)


def _fox_flash(q, qa, kb, ka, vtb, n_seq, dh, tq=1024, tk=1024):
    n, hd = q.shape
    s = n // n_seq
    tq = min(tq, s)
    tk = min(tk, s)
    lanes = FOX_LANES
    assert s % tq == 0 and s % tk == 0 and hd % lanes == 0 and lanes % dh == 0
    nq, nk = s // tq, s // tk
    n_sub = lanes // dh
    qi, kj = _causal_tile_pairs(s, tq, tk)
    q_spec = pl.BlockSpec((tq, lanes), lambda b, g, p, qi, kj: (b * nq + qi[p], g))
    k_spec = pl.BlockSpec((tk, lanes), lambda b, g, p, qi, kj: (b * nk + kj[p], g))
    grid_spec = pltpu.PrefetchScalarGridSpec(
        num_scalar_prefetch=2, grid=(n_seq, hd // lanes, qi.shape[0]),
        in_specs=[q_spec, q_spec, k_spec, k_spec,
                  pl.BlockSpec((1, lanes, tk), lambda b, g, p, qi, kj: (b, g, kj[p]))],
        out_specs=q_spec,
        scratch_shapes=[pltpu.VMEM((n_sub, 1, tq), F32), pltpu.VMEM((n_sub, 1, tq), F32),
                        pltpu.VMEM((lanes, tq), F32)])
    return pl.pallas_call(
        functools.partial(_fox_flash_kernel, tq=tq, tk=tk, dh=dh), name="fox_flash", grid_spec=grid_spec,
        out_shape=jax.ShapeDtypeStruct((n, hd), BF16),
        compiler_params=_cparams("parallel", "parallel", "arbitrary"))(qi, kj, q, qa, kb, ka, vtb)


def _fox_decode_kernel(pt_ref, q_ref, cs_ref, cst_ref, knew_ref, vnew_ref, suf_ref, *rest,
                       n_pp, page, n_heads, t_new):
    kt_refs = rest[:n_pp]
    vt_refs = rest[n_pp:2 * n_pp]
    lft_refs = rest[2 * n_pp:3 * n_pp]
    o_ref = rest[3 * n_pp]
    m_ref, l_ref, acc_ref, carry, kbuf, vbuf = rest[3 * n_pp + 1:]
    j = pl.program_id(1)
    hd = q_ref.shape[2]
    rows = t_new * n_heads
    dh = hd // n_heads

    @pl.when(j == 0)
    def _():
        _init_softmax_state(m_ref, l_ref, acc_ref)
        carry[...] = jnp.zeros(carry.shape, F32)

    own = (lax.broadcasted_iota(jnp.int32, (n_heads, hd), 1) // dh
           == lax.broadcasted_iota(jnp.int32, (n_heads, hd), 0))
    q = q_ref[0]
    qb = jnp.where(own[None], jnp.broadcast_to(q[:, None, :], (t_new, n_heads, hd)), 0.0)
    qb = qb.reshape(rows, hd).astype(BF16)
    cs = cs_ref[0] * LOG2E

    parts = []
    for r in range(n_pp):
        parts += list(_split3(lft_refs[r][0, 0]))
    sums = _dot(jnp.concatenate(parts, axis=0), suf_ref[...])
    cur = carry[...]
    bias = []
    for r in range(n_pp):
        blk = sums[3 * n_heads * r:3 * n_heads * (r + 1), :]
        blk = blk[:n_heads] + blk[n_heads:2 * n_heads] + blk[2 * n_heads:]
        bias.append(blk[:, :page] + cur)
        cur = cur + blk[:, page:page + 1]
        kbuf[:, r * page:(r + 1) * page] = kt_refs[r][0, 0].reshape(hd, page).astype(BF16)
        vbuf[:, r * page:(r + 1) * page] = vt_refs[r][0, 0].reshape(hd, page).astype(BF16)
    carry[...] = cur
    after = jnp.concatenate(bias, axis=1) * LOG2E
    s = _dot(qb, kbuf[...]) + (cs + jnp.concatenate([after] * t_new, axis=0))
    m_prev = m_ref[...]
    m_new = jnp.maximum(m_prev, jnp.max(s, axis=1, keepdims=True))
    alpha = jnp.exp2(m_prev - m_new)
    p = jnp.exp2(s - m_new)
    l_ref[...] = alpha * l_ref[...] + jnp.sum(p, axis=1, keepdims=True)
    m_ref[...] = m_new
    acc_ref[...] = alpha * acc_ref[...] + _dot_nt(p.astype(BF16), vbuf[...])

    @pl.when(j == pl.num_programs(1) - 1)
    def _():
        kn = knew_ref[0]
        s2 = _dot_nt(qb, kn) + (cs - jnp.concatenate([cst_ref[0] * LOG2E] * t_new, axis=0))
        q_t = lax.broadcasted_iota(jnp.int32, (rows, 1), 0) // n_heads
        k_t = lax.broadcasted_iota(jnp.int32, (1, s2.shape[1]), 1)
        s2 = jnp.where(k_t <= q_t, s2, -jnp.inf)
        _online_softmax_step(s2, vnew_ref[0], m_ref, l_ref, acc_ref)
        o_ref[0] = _head_diag(acc_ref[...] / l_ref[...], n_heads, t_new).astype(o_ref.dtype)


def _fox_decode(q_s, cs_rows, cs_t, knew, vnew, cache_kt, cache_vt, cache_lft, layer, page_table):
    db, n_pages = page_table.shape
    n_heads, dh, page = cache_kt.shape[2:]
    hd = n_heads * dh
    t_new = q_s.shape[1]
    n_pp = min(FOX_PAGES_PER_STEP, n_pages)
    assert n_pages % n_pp == 0
    rows = t_new * n_heads
    per_b = lambda b, j, pt: (b, 0, 0)
    suf = jnp.concatenate([jnp.tril(jnp.ones((page, page), F32), -1),
                           jnp.ones((page, 1), F32), jnp.zeros((page, page - 1), F32)], axis=1).astype(BF16)

    def page_spec(r, shape):
        nd = len(shape)
        return pl.BlockSpec((1, 1) + shape,
                            lambda b, j, pt: (layer, pt[b, n_pages - 1 - (j * n_pp + r)]) + (0,) * nd)

    in_specs = ([pl.BlockSpec((1, t_new, hd), per_b), pl.BlockSpec((1, rows, 1), per_b),
                 pl.BlockSpec((1, n_heads, BF16_ROWS), per_b),
                 pl.BlockSpec((1, BF16_ROWS, hd), per_b), pl.BlockSpec((1, BF16_ROWS, hd), per_b),
                 _const_spec(suf.shape)]
                + [page_spec(r, (n_heads, dh, page)) for r in range(n_pp)] * 2
                + [page_spec(r, (n_heads, page)) for r in range(n_pp)])
    grid_spec = pltpu.PrefetchScalarGridSpec(
        num_scalar_prefetch=1, grid=(db, n_pages // n_pp), in_specs=in_specs,
        out_specs=pl.BlockSpec((1, t_new, hd), per_b),
        scratch_shapes=[pltpu.VMEM((rows, 1), F32), pltpu.VMEM((rows, 1), F32), pltpu.VMEM((rows, hd), F32),
                        pltpu.VMEM((n_heads, 1), F32), pltpu.VMEM((hd, n_pp * page), BF16),
                        pltpu.VMEM((hd, n_pp * page), BF16)])
    return pl.pallas_call(
        functools.partial(_fox_decode_kernel, n_pp=n_pp, page=page, n_heads=n_heads, t_new=t_new),
        name="fox_decode",
        grid_spec=grid_spec, out_shape=jax.ShapeDtypeStruct((db, t_new, hd), BF16),
        compiler_params=_cparams("parallel", "arbitrary"))(
            page_table, q_s, cs_rows, cs_t, knew, vnew, suf,
            *([cache_kt] * n_pp), *([cache_vt] * n_pp), *([cache_lft] * n_pp))


def _pad_rows(a, rows):
    return jnp.pad(a, ((0, 0), (0, rows - a.shape[1]), (0, 0)))


def _rope_tables(pos, dr, n_heads):
    inv = ROPE_THETA ** (-jnp.arange(0, dr, 2, dtype=F32) / dr)
    ang = pos.astype(F32)[:, None] * inv[None, :]
    return jnp.tile(jnp.cos(ang), (1, n_heads)), jnp.tile(jnp.sin(ang), (1, n_heads))


def kernel(x_prompt, x_sample, state_b_buf, cache_ckv_c, cache_kr_c, cache_k_d, cache_v_d, cache_logf_d, page_table, g_norm, w_up, w_down, w_in_a, g_v_a, b_v_a, w_s_a, b_s_a, w_out_a, w_in_b, w_grp_b, scale_b, w_in_c, g_q_c, g_kv_c, w_uq_c, w_uk_c, w_uv_c, w_o_c, w_in_d, b_f_d, w_o_d):
    B, S, D = x_prompt.shape
    DB, T, _ = x_sample.shape
    depth = g_norm.shape[0]
    n_pages = page_table.shape[1]
    page = cache_ckv_c.shape[2]
    past = n_pages * page
    pos_p = jnp.arange(S, dtype=jnp.int32)
    pos_s = past + jnp.arange(T, dtype=jnp.int32)
    bf = lambda w: w.astype(BF16)

    xp = x_prompt.reshape(B * S, D)
    xs = x_sample.reshape(DB * T, D)
    outs = {k: [] for k in ("v_a_s", "buf_b_p", "buf_b_s", "ckv_c_p", "kr_c_p", "ckv_c_s", "kr_c_s",
                            "k_d_p", "v_d_p", "logf_d_p", "k_d_s", "v_d_s", "logf_d_s")}
    for i in range(depth):
        kind, j = i % 4, i // 4
        g4 = g_norm[i]
        wu, wd = bf(w_up[i]), bf(w_down[i])
        if kind == 0:
            n_groups, chunk = w_s_a.shape[1], w_s_a.shape[2]
            d_a = w_out_a.shape[1]
            c_a = d_a // n_groups
            assert chunk % T == 0
            tril = jnp.tril(jnp.ones((chunk, chunk), bool))
            ws_p = bf(jnp.where(tril[None], w_s_a[j], 0.0))
            bias_p = jnp.repeat(b_s_a[j].T, c_a, axis=1)
            tril_t = jnp.tril(jnp.ones((T, T), bool))
            ws_t = jnp.where(tril_t[None], w_s_a[j][:, :T, :T], 0.0)
            eye = jnp.eye(chunk // T, dtype=F32)
            ws_s = bf(jnp.einsum("ab,gts->gatbs", eye, ws_t).reshape(n_groups, chunk, chunk))
            bias_s = jnp.tile(jnp.repeat(b_s_a[j][:, :T].T, c_a, axis=1), (chunk // T, 1))
            wa = (g4, bf(w_in_a[j]), g_v_a[j][None], b_v_a[j][None])
            xp, _ = _gmlp(xp, *wa, ws_p, bias_p, bf(w_out_a[j]), chunk)
            xs, v_new = _gmlp(xs, *wa, ws_s, bias_s, bf(w_out_a[j]), chunk)
            outs["v_a_s"].append(v_new.reshape(DB, T, d_a))
            xp = _ffn(xp, g4, wu, wd)
            xs = _ffn(xs, g4, wu, wd)
        elif kind == 1:
            wb = (g4, bf(w_in_b[j]), bf(w_grp_b[j]), scale_b[j][None])
            xp, buf_p = _pool_prompt(xp, *wb, n_seq=B)
            outs["buf_b_p"].append(buf_p[:, HIST_ROWS - POOL_HIST:, :])
            xs_tm = xs.reshape(DB, T, D).transpose(1, 0, 2).reshape(T * DB, D)
            hist_tm = state_b_buf[j].transpose(1, 0, 2).reshape(POOL_HIST * DB, D)
            xs_tm, z_tm = _pool_sample(xs_tm, hist_tm, *wb, db=DB, t_new=T, pos0=past)
            xs = xs_tm.reshape(T, DB, D).transpose(1, 0, 2).reshape(DB * T, D)
            z_s = z_tm.reshape(T, DB, D).transpose(1, 0, 2)
            outs["buf_b_s"].append(jnp.concatenate([state_b_buf[j], z_s], axis=1)[:, -POOL_HIST:])
            xp = _ffn(xp, g4, wu, wd)
            xs = _ffn(xs, g4, wu, wd)
        elif kind == 2:
            n_heads, kv_rank, dn = w_uk_c.shape[1:]
            dv = w_uv_c.shape[3]
            dr = cache_kr_c.shape[3]
            q_rank = g_q_c.shape[1]
            scale = float((dn + dr) ** -0.5) * LOG2E
            wuq3 = w_uq_c[j].reshape(q_rank, n_heads, dn + dr)
            wuq_perm = bf(jnp.concatenate([wuq3[:, :, :dn].reshape(q_rank, -1),
                                           wuq3[:, :, dn:dn + dr // 2].reshape(q_rank, -1),
                                           wuq3[:, :, dn + dr // 2:].reshape(q_rank, -1)], axis=1))
            wukt = bf(w_uk_c[j].transpose(0, 2, 1))
            wuvt = bf(w_uv_c[j].transpose(0, 2, 1))
            wuv_all = bf(w_uv_c[j].transpose(1, 0, 2).reshape(kv_rank, n_heads * dv))
            wc = (g4, bf(w_in_c[j]), g_q_c[j][None], g_kv_c[j][None], wuq_perm, wukt)
            cos_p, sin_p = _rope_tables(jnp.tile(pos_p, B), dr, n_heads)
            q_cat, kcat, ckvt, ckv, kr = _mla_proj(xp, *wc, cos_p, sin_p, scale)
            o_p = _mla_flash(q_cat, kcat, ckvt, wuvt, n_seq=B)
            outs["ckv_c_p"].append(ckv.reshape(B, S, kv_rank))
            outs["kr_c_p"].append(kr.reshape(B, S, dr))
            cos_s, sin_s = _rope_tables(jnp.tile(pos_s, DB), dr, n_heads)
            q_cat_s, kcat_s, _, ckv_s, kr_s = _mla_proj(xs, *wc, cos_s, sin_s, scale)
            dk = kv_rank + dr
            q_rows = q_cat_s.reshape(n_heads, DB, T, dk).transpose(1, 2, 0, 3).reshape(DB * T * n_heads, dk)
            knew = _pad_rows(kcat_s.reshape(DB, T, dk), BF16_ROWS)
            o_s = _mla_decode(q_rows, knew, wuv_all, cache_ckv_c, cache_kr_c.transpose(0, 1, 3, 2), j,
                              page_table, T, n_heads)
            outs["ckv_c_s"].append(ckv_s.reshape(DB, T, kv_rank))
            outs["kr_c_s"].append(kr_s.reshape(DB, T, dr))
            xp = _ffn(xp, g4, wu, wd, a=o_p, wo=bf(w_o_c[j]))
            xs = _ffn(xs, g4, wu, wd, a=o_s.reshape(DB * T, n_heads * dv), wo=bf(w_o_c[j]))
        else:
            n_heads, dh = cache_k_d.shape[3], cache_k_d.shape[4]
            hd = n_heads * dh
            scale = float(dh ** -0.5) * LOG2E
            wdd = (g4, bf(w_in_d[j]), b_f_d[j][None], scale)
            q, lf, _, kt, vt, kb, vtb, qa, ka = _fox_proj(xp, *wdd, n_seq=B)
            o_p = _fox_flash(q, qa, kb, ka, vtb, n_seq=B, dh=dh)
            outs["k_d_p"].append(kt.reshape(B, n_heads, dh, S).transpose(0, 3, 1, 2))
            outs["v_d_p"].append(vt.reshape(B, n_heads, dh, S).transpose(0, 3, 1, 2))
            outs["logf_d_p"].append(lf.reshape(B, S, n_heads))
            q_s, lf_s, cs, k_s, v_s, kb_s, vb_s = _fox_proj(xs, *wdd, seg=T)
            cs3 = cs.reshape(DB, T, n_heads)
            cs_t = jnp.pad(cs3.transpose(0, 2, 1), ((0, 0), (0, 0), (0, BF16_ROWS - T)))
            o_s = _fox_decode(q_s.astype(F32).reshape(DB, T, hd), cs3.reshape(DB, T * n_heads, 1), cs_t,
                              _pad_rows(kb_s.reshape(DB, T, hd), BF16_ROWS),
                              _pad_rows(vb_s.reshape(DB, T, hd), BF16_ROWS),
                              cache_k_d.transpose(0, 1, 3, 4, 2), cache_v_d.transpose(0, 1, 3, 4, 2),
                              cache_logf_d.transpose(0, 1, 3, 2), j, page_table)
            outs["k_d_s"].append(k_s.reshape(DB, T, n_heads, dh))
            outs["v_d_s"].append(v_s.reshape(DB, T, n_heads, dh))
            outs["logf_d_s"].append(lf_s.reshape(DB, T, n_heads))
            xp = _ffn(xp, g4, wu, wd, a=o_p, wo=bf(w_o_d[j]))
            xs = _ffn(xs, g4, wu, wd, a=o_s.reshape(DB * T, hd), wo=bf(w_o_d[j]))
    st = lambda name: jnp.stack(outs[name])
    return (xp.reshape(B, S, D), xs.reshape(DB, T, D), st("v_a_s"), st("buf_b_p"), st("buf_b_s"),
            st("ckv_c_p"), st("kr_c_p"), st("ckv_c_s"), st("kr_c_s"),
            st("k_d_p"), st("v_d_p"), st("logf_d_p"), st("k_d_s"), st("v_d_s"), st("logf_d_s"))
```

```python
import functools

import jax
import jax.numpy as jnp
import numpy as np
from jax import lax
from jax.experimental import pallas as pl
from jax.experimental.pallas import tpu as pltpu

F32 = jnp.float32
BF16 = jnp.bfloat16
EPS = 1e-6
LOG2E = 1.4426950408889634
ROPE_THETA = 10000.0
POOL_WINDOWS = (2, 4, 8, 16)
POOL_HIST = max(POOL_WINDOWS) - 1
HIST_ROWS = 16
BF16_ROWS = 16
V7X_VMEM_LIMIT = 56 * 1024 * 1024
FOX_PAGES_PER_STEP = 16
MLA_PAGES_PER_STEP = 32
MLA_CHUNK_COLS = 4096
FOX_CHUNK_COLS = 1024


def _cparams(*sem):
    return pltpu.CompilerParams(dimension_semantics=sem, vmem_limit_bytes=V7X_VMEM_LIMIT)


def _const_spec(shape):
    nd = len(shape)
    return pl.BlockSpec(shape, lambda *_: (0,) * nd, pipeline_mode=pl.Buffered(1))


def _rms(x, g):
    return x * lax.rsqrt(jnp.mean(x * x, axis=-1, keepdims=True) + EPS) * g


def _dot(a, b):
    return jnp.dot(a, b, preferred_element_type=F32)


def _dot_nt(a, b):
    return lax.dot_general(a, b, (((1,), (1,)), ((), ())), preferred_element_type=F32)


def _split3(x):
    hi = x.astype(BF16)
    r = x - hi.astype(F32)
    mid = r.astype(BF16)
    lo = (r - mid.astype(F32)).astype(BF16)
    return hi, mid, lo


def _cumsum_rows(x, seg=None):
    n = x.shape[0]
    r = lax.broadcasted_iota(jnp.int32, (n, n), 0)
    c = lax.broadcasted_iota(jnp.int32, (n, n), 1)
    keep = c <= r
    if seg is not None and seg < n:
        keep = keep & (c // seg == r // seg)
    tri = jnp.where(keep, 1.0, 0.0).astype(BF16)
    hi, mid, lo = _split3(x)
    return _dot(tri, hi) + _dot(tri, mid) + _dot(tri, lo)


def _online_softmax_cols(st, vt, m_prev, l_prev):
    m_new = jnp.maximum(m_prev, jnp.max(st, axis=0, keepdims=True))
    alpha = jnp.exp2(m_prev - m_new)
    p = jnp.exp2(st - m_new)
    l_new = alpha * l_prev + jnp.sum(p, axis=0, keepdims=True)
    return m_new, l_new, alpha, _dot(vt, p.astype(BF16))


def _online_softmax_step(s, v, m_ref, l_ref, acc_ref):
    m_prev = m_ref[...]
    m_new = jnp.maximum(m_prev, jnp.max(s, axis=1, keepdims=True))
    alpha = jnp.exp2(m_prev - m_new)
    p = jnp.exp2(s - m_new)
    l_ref[...] = alpha * l_ref[...] + jnp.sum(p, axis=1, keepdims=True)
    acc_ref[...] = alpha * acc_ref[...] + _dot(p.astype(BF16), v)
    m_ref[...] = m_new


def _init_softmax_state(m_ref, l_ref, acc_ref):
    m_ref[...] = jnp.full(m_ref.shape, -jnp.inf, F32)
    l_ref[...] = jnp.zeros(l_ref.shape, F32)
    acc_ref[...] = jnp.zeros(acc_ref.shape, F32)


def _ffn_kernel(*refs, fc, with_proj):
    if with_proj:
        a_ref, x_ref, wo_ref, g_ref, wu_ref, wd_ref, o_ref = refs
        x1 = x_ref[...] + _rms(_dot(a_ref[...], wo_ref[...]), g_ref[1:2, :])
    else:
        x_ref, g_ref, wu_ref, wd_ref, o_ref = refs
        x1 = x_ref[...]
    h = _rms(x1, g_ref[2:3, :]).astype(BF16)
    acc = None
    for c in range(wu_ref.shape[1] // fc):
        u = jnp.maximum(_dot(h, wu_ref[:, c * fc:(c + 1) * fc]), 0.0)
        d = _dot((u * u).astype(BF16), wd_ref[c * fc:(c + 1) * fc, :])
        acc = d if acc is None else acc + d
    o_ref[...] = x1 + _rms(acc, g_ref[3:4, :])


def _ffn(x, g4, wu, wd, a=None, wo=None, tm=512):
    n, d = x.shape
    tm = min(tm, n)
    assert n % tm == 0
    f = wu.shape[1]
    fc = min(1024, f)
    assert f % fc == 0
    row = lambda i: (i, 0)
    specs, args = [], []
    if a is not None:
        specs += [pl.BlockSpec((tm, a.shape[1]), row), pl.BlockSpec((tm, d), row), _const_spec(wo.shape)]
        args += [a, x, wo]
    else:
        specs += [pl.BlockSpec((tm, d), row)]
        args += [x]
    specs += [_const_spec(g4.shape), _const_spec(wu.shape), _const_spec(wd.shape)]
    args += [g4, wu, wd]
    return pl.pallas_call(
        functools.partial(_ffn_kernel, fc=fc, with_proj=a is not None), name="ffn",
        grid=(n // tm,), in_specs=specs, out_specs=pl.BlockSpec((tm, d), row),
        out_shape=jax.ShapeDtypeStruct((n, d), F32), compiler_params=_cparams("parallel"))(*args)


def _gmlp_kernel(x_ref, g_ref, win_ref, gv_ref, bv_ref, ws_ref, bias_ref, wout_ref, o_ref, v_ref, *, chunk):
    x = x_ref[...]
    d_a = wout_ref.shape[0]
    n_groups = ws_ref.shape[0]
    c_a = d_a // n_groups
    uv = jax.nn.gelu(_dot(_rms(x, g_ref[0:1, :]).astype(BF16), win_ref[...]))
    u = uv[:, :d_a]
    v = uv[:, d_a:]
    vc = v - jnp.mean(v, axis=-1, keepdims=True)
    v = vc * lax.rsqrt(jnp.mean(vc * vc, axis=-1, keepdims=True) + EPS) * gv_ref[...] + bv_ref[...]
    v_ref[...] = v
    vb = v.astype(BF16)
    rows = []
    for c in range(x.shape[0] // chunk):
        cols = [_dot(ws_ref[g], vb[c * chunk:(c + 1) * chunk, g * c_a:(g + 1) * c_a]) for g in range(n_groups)]
        rows.append(jnp.concatenate(cols, axis=1))
    mixed = jnp.concatenate(rows, axis=0) + bias_ref[...]
    m = _dot((u * mixed).astype(BF16), wout_ref[...])
    o_ref[...] = x + _rms(m, g_ref[1:2, :])


def _gmlp(x, g4, w_in, g_v, b_v, ws_eff, bias_rows, w_out, chunk, tm=256):
    n, d = x.shape
    tm = min(tm, n)
    assert n % tm == 0 and tm % chunk == 0
    d_a = w_out.shape[0]
    bias = jnp.tile(bias_rows, (tm // chunk, 1))
    row = lambda i: (i, 0)
    return pl.pallas_call(
        functools.partial(_gmlp_kernel, chunk=chunk), name="gmlp",
        grid=(n // tm,),
        in_specs=[pl.BlockSpec((tm, d), row), _const_spec(g4.shape), _const_spec(w_in.shape),
                  _const_spec(g_v.shape), _const_spec(b_v.shape), _const_spec(ws_eff.shape),
                  _const_spec(bias.shape), _const_spec(w_out.shape)],
        out_specs=[pl.BlockSpec((tm, d), row), pl.BlockSpec((tm, d_a), row)],
        out_shape=[jax.ShapeDtypeStruct((n, d), F32), jax.ShapeDtypeStruct((n, d_a), F32)],
        compiler_params=_cparams("parallel"))(x, g4, w_in, g_v, b_v, ws_eff, bias, w_out)


def _pool_finish(x, dmat, g_ref, wg_ref, sc_ref):
    n_g = wg_ref.shape[0]
    cb = x.shape[1] // n_g
    ys = [_dot(dmat[:, gi * cb:(gi + 1) * cb], wg_ref[gi]) for gi in range(n_g)]
    y = jnp.concatenate(ys, axis=1) * sc_ref[...]
    return x + _rms(y, g_ref[1:2, :])


def _pool_prompt_kernel(x_ref, g_ref, win_ref, wg_ref, sc_ref, o_ref, buf_ref, zbuf, *, tiles_per_seq):
    tm, d = x_ref.shape
    li = pl.program_id(0) % tiles_per_seq

    @pl.when(li == 0)
    def _():
        zbuf[0:HIST_ROWS, :] = jnp.zeros((HIST_ROWS, d), F32)

    x = x_ref[...]
    z = _dot(_rms(x, g_ref[0:1, :]).astype(BF16), win_ref[...])
    zbuf[HIST_ROWS:HIST_ROWS + tm, :] = z
    pos = li * tm + lax.broadcasted_iota(jnp.int32, (tm, 1), 0)
    cb = d // len(POOL_WINDOWS)
    cols = []
    for gi, w in enumerate(POOL_WINDOWS):
        sl = slice(gi * cb, (gi + 1) * cb)
        zg = z[:, sl]
        acc = zg
        for k in range(1, w):
            acc = acc + zbuf[HIST_ROWS - k:HIST_ROWS - k + tm, sl]
        cnt = jnp.minimum(w, pos + 1).astype(F32)
        cols.append(acc / cnt - zg)
    dmat = jnp.concatenate(cols, axis=1).astype(BF16)
    o_ref[...] = _pool_finish(x, dmat, g_ref, wg_ref, sc_ref)
    tail = z[tm - HIST_ROWS:, :]
    zbuf[0:HIST_ROWS, :] = tail
    buf_ref[0] = tail


def _pool_prompt(x, g4, w_in, w_grp, scale, n_seq, tm=256):
    n, d = x.shape
    s = n // n_seq
    tm = min(tm, s)
    assert s % tm == 0 and tm >= HIST_ROWS
    tps = s // tm
    row = lambda i: (i, 0)
    return pl.pallas_call(
        functools.partial(_pool_prompt_kernel, tiles_per_seq=tps), name="pool_prompt",
        grid=(n // tm,),
        in_specs=[pl.BlockSpec((tm, d), row), _const_spec(g4.shape), _const_spec(w_in.shape),
                  _const_spec(w_grp.shape), _const_spec(scale.shape)],
        out_specs=[pl.BlockSpec((tm, d), row), pl.BlockSpec((1, HIST_ROWS, d), lambda i: (i // tps, 0, 0))],
        out_shape=[jax.ShapeDtypeStruct((n, d), F32), jax.ShapeDtypeStruct((n_seq, HIST_ROWS, d), F32)],
        scratch_shapes=[pltpu.VMEM((HIST_ROWS + tm, d), F32)],
        compiler_params=_cparams("arbitrary"))(x, g4, w_in, w_grp, scale)


def _pool_sample_kernel(x_ref, hist_ref, g_ref, win_ref, wg_ref, sc_ref, o_ref, z_ref, *, db, t_new, pos0):
    x = x_ref[...]
    d = x.shape[1]
    z = _dot(_rms(x, g_ref[0:1, :]).astype(BF16), win_ref[...])
    z_ref[...] = z
    cb = d // len(POOL_WINDOWS)

    def rows_at(p, sl):
        if p >= POOL_HIST:
            return z[(p - POOL_HIST) * db:(p - POOL_HIST + 1) * db, sl]
        return hist_ref[p * db:(p + 1) * db, sl]

    drows = []
    for t in range(t_new):
        cols = []
        for gi, w in enumerate(POOL_WINDOWS):
            sl = slice(gi * cb, (gi + 1) * cb)
            acc = rows_at(POOL_HIST + t, sl)
            for k in range(1, w):
                acc = acc + rows_at(POOL_HIST + t - k, sl)
            cnt = float(min(w, pos0 + t + 1))
            cols.append(acc / cnt - z[t * db:(t + 1) * db, sl])
        drows.append(jnp.concatenate(cols, axis=1))
    dmat = jnp.concatenate(drows, axis=0).astype(BF16)
    o_ref[...] = _pool_finish(x, dmat, g_ref, wg_ref, sc_ref)


def _pool_sample(x_tm, hist_tm, g4, w_in, w_grp, scale, db, t_new, pos0):
    n, d = x_tm.shape
    args = (x_tm, hist_tm, g4, w_in, w_grp, scale)
    return pl.pallas_call(
        functools.partial(_pool_sample_kernel, db=db, t_new=t_new, pos0=pos0), name="pool_sample",
        grid=(1,),
        in_specs=[_const_spec(a.shape) for a in args],
        out_specs=[pl.BlockSpec((n, d), lambda i: (0, 0)), pl.BlockSpec((n, d), lambda i: (0, 0))],
        out_shape=[jax.ShapeDtypeStruct((n, d), F32), jax.ShapeDtypeStruct((n, d), F32)],
        compiler_params=_cparams("arbitrary"))(*args)


def _mla_proj_kernel(x_ref, g_ref, win_ref, gq_ref, gkv_ref, wuq_ref, wukt_ref, cos_ref, sin_ref,
                     q_ref, kcat_ref, ckvt_ref, ckv_ref, kr_ref, *, q_rank, kv_rank, scale):
    n_heads, dn, _ = wukt_ref.shape
    x = x_ref[...]
    c = _dot(_rms(x, g_ref[0:1, :]).astype(BF16), win_ref[...])
    c_q = c[:, :q_rank]
    c_kv = c[:, q_rank:q_rank + kv_rank]
    k_r = c[:, q_rank + kv_rank:]
    q = _dot(_rms(c_q, gq_ref[...]).astype(BF16), wuq_ref[...])
    hd = n_heads * dn
    half = (q.shape[1] - hd) // 2
    hr = half // n_heads
    x1 = q[:, hd:hd + half]
    x2 = q[:, hd + half:]
    cosq = cos_ref[...]
    sinq = sin_ref[...]
    r1 = ((x1 * cosq - x2 * sinq) * scale).astype(BF16)
    r2 = ((x1 * sinq + x2 * cosq) * scale).astype(BF16)
    ckv = _rms(c_kv, gkv_ref[...])
    ckv_ref[...] = ckv
    ckvt_ref[...] = ckv.T.astype(BF16)
    k1 = k_r[:, :hr]
    k2 = k_r[:, hr:]
    c1 = cosq[:, :hr]
    s1 = sinq[:, :hr]
    kr = jnp.concatenate([k1 * c1 - k2 * s1, k1 * s1 + k2 * c1], axis=1)
    kr_ref[...] = kr
    kcat_ref[:, :kv_rank] = ckv.astype(BF16)
    kcat_ref[:, kv_rank:] = kr.astype(BF16)
    for h in range(n_heads):
        ql = _dot(q[:, h * dn:(h + 1) * dn].astype(BF16), wukt_ref[h]) * scale
        q_ref[h, :, :kv_rank] = ql.astype(BF16)
        q_ref[h, :, kv_rank:kv_rank + hr] = r1[:, h * hr:(h + 1) * hr]
        q_ref[h, :, kv_rank + hr:] = r2[:, h * hr:(h + 1) * hr]


def _mla_proj(x, g4, w_in, g_q, g_kv, wuq_perm, wukt, cosq, sinq, scale, tm=256):
    n, d = x.shape
    tm = min(tm, n)
    assert n % tm == 0
    n_heads, _, kv_rank = wukt.shape
    q_rank = g_q.shape[1]
    dr = w_in.shape[1] - q_rank - kv_rank
    dk = kv_rank + dr
    row = lambda i: (i, 0)
    return pl.pallas_call(
        functools.partial(_mla_proj_kernel, q_rank=q_rank, kv_rank=kv_rank, scale=scale), name="mla_proj",
        grid=(n // tm,),
        in_specs=[pl.BlockSpec((tm, d), row), _const_spec(g4.shape), _const_spec(w_in.shape),
                  _const_spec(g_q.shape), _const_spec(g_kv.shape), _const_spec(wuq_perm.shape),
                  _const_spec(wukt.shape), pl.BlockSpec((tm, cosq.shape[1]), row),
                  pl.BlockSpec((tm, sinq.shape[1]), row)],
        out_specs=[pl.BlockSpec((n_heads, tm, dk), lambda i: (0, i, 0)), pl.BlockSpec((tm, dk), row),
                   pl.BlockSpec((kv_rank, tm), lambda i: (0, i)),
                   pl.BlockSpec((tm, kv_rank), row), pl.BlockSpec((tm, dr), row)],
        out_shape=[jax.ShapeDtypeStruct((n_heads, n, dk), BF16), jax.ShapeDtypeStruct((n, dk), BF16),
                   jax.ShapeDtypeStruct((kv_rank, n), BF16),
                   jax.ShapeDtypeStruct((n, kv_rank), F32), jax.ShapeDtypeStruct((n, dr), F32)],
        compiler_params=_cparams("parallel"))(x, g4, w_in, g_q, g_kv, wuq_perm, wukt, cosq, sinq)


def _causal_tile_pairs(s, tq, tk):
    pairs = [(i, j) for i in range(s // tq) for j in range((i * tq + tq - 1) // tk + 1)]
    return (jnp.asarray([p[0] for p in pairs], jnp.int32), jnp.asarray([p[1] for p in pairs], jnp.int32))


def _mla_flash_kernel(qi_ref, kj_ref, q_ref, k_ref, vt_ref, wuvt_ref, o_ref, m_ref, l_ref, acc_ref, *, tq, tk):
    p = pl.program_id(1)
    i = qi_ref[p]
    j = kj_ref[p]
    n_heads = q_ref.shape[0]
    m_cols = n_heads * tq

    @pl.when(j == 0)
    def _():
        _init_softmax_state(m_ref, l_ref, acc_ref)

    def step(masked):
        k = k_ref[...]
        vt = vt_ref[...]
        hpc = max(1, min(n_heads, MLA_CHUNK_COLS // tq))
        cw = hpc * tq
        if masked:
            q_pos = i * tq + (lax.broadcasted_iota(jnp.int32, (1, cw), 1) & (tq - 1))
            causal = j * tk + lax.broadcasted_iota(jnp.int32, (tk, 1), 0) <= q_pos
        for c in range(n_heads // hpc):
            cols = slice(c * cw, (c + 1) * cw)
            st = _dot_nt(k, q_ref[c * hpc:(c + 1) * hpc].reshape(cw, q_ref.shape[2]))
            if masked:
                st = jnp.where(causal, st, -jnp.inf)
            m_new, l_new, alpha, pv = _online_softmax_cols(st, vt, m_ref[:, cols], l_ref[:, cols])
            m_ref[:, cols] = m_new
            l_ref[:, cols] = l_new
            acc_ref[:, cols] = alpha * acc_ref[:, cols] + pv

    crosses_diagonal = j * tk + tk - 1 > i * tq

    @pl.when(crosses_diagonal)
    def _():
        step(True)

    @pl.when(jnp.logical_not(crosses_diagonal))
    def _():
        step(False)

    @pl.when((j + 1) * tk > i * tq + tq - 1)
    def _():
        ot = (acc_ref[...] / l_ref[...]).astype(BF16)
        outs = [_dot(wuvt_ref[h], ot[:, h * tq:(h + 1) * tq]) for h in range(n_heads)]
        o_ref[...] = jnp.concatenate(outs, axis=0).T.astype(o_ref.dtype)


def _mla_flash(q_cat, kcat, ckvt, wuvt, n_seq, tq=256, tk=512):
    n_heads, n, dk = q_cat.shape
    s = n // n_seq
    tq = min(tq, s)
    tk = min(tk, s)
    assert s % tq == 0 and s % tk == 0 and tq & (tq - 1) == 0
    nq, nk = s // tq, s // tk
    dv, kv_rank = wuvt.shape[1], wuvt.shape[2]
    qi, kj = _causal_tile_pairs(s, tq, tk)
    grid_spec = pltpu.PrefetchScalarGridSpec(
        num_scalar_prefetch=2, grid=(n_seq, qi.shape[0]),
        in_specs=[pl.BlockSpec((n_heads, tq, dk), lambda b, p, qi, kj: (0, b * nq + qi[p], 0)),
                  pl.BlockSpec((tk, dk), lambda b, p, qi, kj: (b * nk + kj[p], 0)),
                  pl.BlockSpec((kv_rank, tk), lambda b, p, qi, kj: (0, b * nk + kj[p])),
                  _const_spec(wuvt.shape)],
        out_specs=pl.BlockSpec((tq, n_heads * dv), lambda b, p, qi, kj: (b * nq + qi[p], 0)),
        scratch_shapes=[pltpu.VMEM((1, n_heads * tq), F32), pltpu.VMEM((1, n_heads * tq), F32),
                        pltpu.VMEM((kv_rank, n_heads * tq), F32)])
    return pl.pallas_call(
        functools.partial(_mla_flash_kernel, tq=tq, tk=tk), name="mla_flash", grid_spec=grid_spec,
        out_shape=jax.ShapeDtypeStruct((n, n_heads * dv), BF16),
        compiler_params=_cparams("parallel", "arbitrary"))(qi, kj, q_cat, kcat, ckvt, wuvt)


def _head_diag(full, n_heads, t_new):
    width = full.shape[1]
    dv = width // n_heads
    own = (lax.broadcasted_iota(jnp.int32, (n_heads, width), 1) // dv
           == lax.broadcasted_iota(jnp.int32, (n_heads, width), 0))
    full3 = full.reshape(t_new, n_heads, width)
    return jnp.sum(jnp.where(own[None], full3, 0.0), axis=1)


def _mla_decode_kernel(pt_ref, q_ref, knew_ref, wuv_ref, *rest, n_b, n_pp, page, kv_rank, t_new, n_heads):
    n_pg = n_b * n_pp
    ckv_refs = rest[:n_pg]
    krt_refs = rest[n_pg:2 * n_pg]
    o_ref = rest[2 * n_pg]
    scratch = rest[2 * n_pg + 1:]
    state = [scratch[5 * u:5 * u + 5] for u in range(n_b)]
    j = pl.program_id(1)
    rows = t_new * n_heads

    @pl.when(j == 0)
    def _():
        for m_ref, l_ref, acc_ref, _, _ in state:
            _init_softmax_state(m_ref, l_ref, acc_ref)

    for u, (m_ref, l_ref, acc_ref, cbuf, rbuf) in enumerate(state):
        for r in range(n_pp):
            cbuf[r * page:(r + 1) * page, :] = ckv_refs[u * n_pp + r][0, 0].astype(BF16)
            rbuf[:, r * page:(r + 1) * page] = krt_refs[u * n_pp + r][0, 0].astype(BF16)
        q = q_ref[u * rows:(u + 1) * rows, :]
        cb = cbuf[...]
        s = _dot_nt(q[:, :kv_rank], cb) + _dot(q[:, kv_rank:], rbuf[...])
        _online_softmax_step(s, cb, m_ref, l_ref, acc_ref)

    @pl.when(j == pl.num_programs(1) - 1)
    def _():
        for u, (m_ref, l_ref, acc_ref, _, _) in enumerate(state):
            q = q_ref[u * rows:(u + 1) * rows, :]
            kn = knew_ref[u]
            s = _dot_nt(q, kn)
            q_t = lax.broadcasted_iota(jnp.int32, (s.shape[0], 1), 0) // n_heads
            k_t = lax.broadcasted_iota(jnp.int32, (1, s.shape[1]), 1)
            s = jnp.where(k_t <= q_t, s, -jnp.inf)
            _online_softmax_step(s, kn[:, :kv_rank], m_ref, l_ref, acc_ref)
            ol = (acc_ref[...] / l_ref[...]).astype(BF16)
            o_ref[u] = _head_diag(_dot(ol, wuv_ref[...]), n_heads, t_new).astype(o_ref.dtype)


def _mla_decode(q_rows, knew, wuv_all, cache_ckv, cache_krt, layer, page_table, t_new, n_heads, n_b=2):
    db, n_pages = page_table.shape
    page, kv_rank = cache_ckv.shape[2], cache_ckv.shape[3]
    dr = cache_krt.shape[2]
    dk = kv_rank + dr
    n_pp = min(MLA_PAGES_PER_STEP, n_pages)
    assert n_pages % n_pp == 0 and db % n_b == 0
    rows = t_new * n_heads
    width = wuv_all.shape[1]

    def page_spec(u, r, shape):
        return pl.BlockSpec((1, 1) + shape, lambda b, j, pt: (layer, pt[b * n_b + u, j * n_pp + r], 0, 0))

    in_specs = ([pl.BlockSpec((n_b * rows, dk), lambda b, j, pt: (b, 0)),
                 pl.BlockSpec((n_b, BF16_ROWS, dk), lambda b, j, pt: (b, 0, 0)),
                 _const_spec(wuv_all.shape)]
                + [page_spec(u, r, (page, kv_rank)) for u in range(n_b) for r in range(n_pp)]
                + [page_spec(u, r, (dr, page)) for u in range(n_b) for r in range(n_pp)])
    per_sample = [pltpu.VMEM((rows, 1), F32), pltpu.VMEM((rows, 1), F32), pltpu.VMEM((rows, kv_rank), F32),
                  pltpu.VMEM((n_pp * page, kv_rank), BF16), pltpu.VMEM((dr, n_pp * page), BF16)]
    grid_spec = pltpu.PrefetchScalarGridSpec(
        num_scalar_prefetch=1, grid=(db // n_b, n_pages // n_pp), in_specs=in_specs,
        out_specs=pl.BlockSpec((n_b, t_new, width), lambda b, j, pt: (b, 0, 0)),
        scratch_shapes=per_sample * n_b)
    return pl.pallas_call(
        functools.partial(_mla_decode_kernel, n_b=n_b, n_pp=n_pp, page=page, kv_rank=kv_rank, t_new=t_new,
                          n_heads=n_heads),
        name="mla_decode",
        grid_spec=grid_spec, out_shape=jax.ShapeDtypeStruct((db, t_new, width), BF16),
        compiler_params=_cparams("parallel", "arbitrary"))(
            page_table, q_rows, knew, wuv_all, *([cache_ckv] * (n_b * n_pp)), *([cache_krt] * (n_b * n_pp)))


def _fox_proj_kernel(x_ref, g_ref, win_ref, bf_ref, place_ref, ones_ref, q_ref, lf_ref, c_ref, *rest,
                     hd, scale, seg, tiles_per_seq):
    x = x_ref[...]
    y = _dot(_rms(x, g_ref[0:1, :]).astype(BF16), win_ref[...])
    q_ref[...] = (y[:, :hd] * scale).astype(BF16)
    k = y[:, hd:2 * hd]
    v = y[:, 2 * hd:3 * hd]
    lf = jax.nn.log_sigmoid(y[:, 3 * hd:] + bf_ref[...])
    lf_ref[...] = lf
    c = _cumsum_rows(lf, seg)
    if tiles_per_seq is None:
        k_ref, v_ref, kb_ref, vb_ref = rest
        k_ref[...] = k
        v_ref[...] = v
        kb_ref[...] = k.astype(BF16)
        vb_ref[...] = v.astype(BF16)
    else:
        kt_ref, vt_ref, kb_ref, vtb_ref, qa_ref, ka_ref, carry = rest

        @pl.when(pl.program_id(0) % tiles_per_seq == 0)
        def _():
            carry[...] = jnp.zeros(carry.shape, F32)

        c = c + carry[...]
        carry[...] = c[c.shape[0] - 1:, :]
        vt = v.T
        kt_ref[0] = k.T
        vt_ref[0] = vt
        kb_ref[...] = k.astype(BF16)
        vtb_ref[0] = vt.astype(BF16)
        n_heads = c.shape[1]
        parts = _split3(c * LOG2E)
        for side, out_ref in ((0, qa_ref), (1, ka_ref)):
            aug = ones_ref[side:side + 1, :]
            for t, part in enumerate(parts):
                aug = aug + _dot(part, place_ref[side, t * n_heads:(t + 1) * n_heads, :])
            out_ref[...] = aug.astype(BF16)
    c_ref[...] = c


def _fox_proj(x, g4, w_in, b_f, scale, n_seq=None, seg=None, tm=256):
    n, d = x.shape
    n_heads = b_f.shape[1]
    hd = (w_in.shape[1] - n_heads) // 3
    s = n // n_seq if n_seq else n
    tm = min(tm, s)
    assert s % tm == 0 and (n_seq or tm % seg == 0)
    tps = s // tm
    row = lambda i: (i, 0)
    wide = pl.BlockSpec((tm, hd), row)
    narrow = pl.BlockSpec((tm, n_heads), row)
    out_specs = [wide, narrow, narrow]
    out_shape = [jax.ShapeDtypeStruct((n, hd), BF16), jax.ShapeDtypeStruct((n, n_heads), F32),
                 jax.ShapeDtypeStruct((n, n_heads), F32)]
    scratch = []
    if n_seq:
        tspec = pl.BlockSpec((1, hd, tm), lambda i: (i // tps, 0, i % tps))
        out_specs += [tspec, tspec, wide, tspec, wide, wide]
        out_shape += [jax.ShapeDtypeStruct((n_seq, hd, s), F32), jax.ShapeDtypeStruct((n_seq, hd, s), F32),
                      jax.ShapeDtypeStruct((n, hd), BF16), jax.ShapeDtypeStruct((n_seq, hd, s), BF16),
                      jax.ShapeDtypeStruct((n, hd), BF16), jax.ShapeDtypeStruct((n, hd), BF16)]
        scratch.append(pltpu.VMEM((1, n_heads), F32))
    else:
        out_specs += [wide, wide, wide, wide]
        out_shape += [jax.ShapeDtypeStruct((n, hd), F32), jax.ShapeDtypeStruct((n, hd), F32),
                      jax.ShapeDtypeStruct((n, hd), BF16), jax.ShapeDtypeStruct((n, hd), BF16)]
    place, ones = _fox_aug_tables(n_heads, hd // n_heads)
    return pl.pallas_call(
        functools.partial(_fox_proj_kernel, hd=hd, scale=scale, seg=seg, tiles_per_seq=tps if n_seq else None),
        name="fox_proj",
        grid=(n // tm,),
        in_specs=[pl.BlockSpec((tm, d), row), _const_spec(g4.shape), _const_spec(w_in.shape),
                  _const_spec(b_f.shape), _const_spec(place.shape), _const_spec(ones.shape)],
        out_specs=out_specs, out_shape=out_shape, scratch_shapes=scratch,
        compiler_params=_cparams("arbitrary"))(x, g4, w_in, b_f, place, ones)


FOX_LANES = 128
FOX_AUG_STRIDE = 8


def _fox_aug_tables(n_heads, dh):
    n_sub = FOX_LANES // dh
    place = np.zeros((2, 3 * n_heads, n_heads * dh), np.float32)
    ones = np.zeros((2, n_heads * dh), np.float32)
    for head in range(n_heads):
        base = (head // n_sub) * FOX_LANES + (head % n_sub) * FOX_AUG_STRIDE
        for t in range(3):
            place[0, t * n_heads + head, base + 3 + t] = 1.0
            place[1, t * n_heads + head, base + t] = -1.0
            ones[0, base + t] = 1.0
            ones[1, base + 3 + t] = 1.0
    return jnp.asarray(place, BF16), jnp.asarray(ones, F32)


def _fox_flash_kernel(qi_ref, kj_ref, q_ref, qa_ref, k_ref, ka_ref, vt_ref, o_ref, m_ref, l_ref, acc_ref,
                      *, tq, tk, dh):
    p = pl.program_id(2)
    i = qi_ref[p]
    j = kj_ref[p]
    width = q_ref.shape[1]
    n_sub = width // dh
    lane = lax.broadcasted_iota(jnp.int32, (1, 2 * width), 1)

    @pl.when(j == 0)
    def _():
        _init_softmax_state(m_ref, l_ref, acc_ref)

    def step(masked):
        k = jnp.concatenate([k_ref[...], ka_ref[...]], axis=1)
        cw = min(tq, FOX_CHUNK_COLS)
        for hh in range(n_sub):
            own = ((lane // dh == hh)
                   | ((lane >= width + hh * FOX_AUG_STRIDE) & (lane < width + (hh + 1) * FOX_AUG_STRIDE)))
            km = jnp.where(own, k, jnp.zeros_like(k))
            rows = slice(hh * dh, (hh + 1) * dh)
            vt = vt_ref[0, rows, :]
            for c in range(tq // cw):
                cols = slice(c * cw, (c + 1) * cw)
                st = _dot_nt(km, jnp.concatenate([q_ref[cols, :], qa_ref[cols, :]], axis=1))
                if masked:
                    causal = (j * tk + lax.broadcasted_iota(jnp.int32, (tk, 1), 0)
                              <= i * tq + c * cw + lax.broadcasted_iota(jnp.int32, (1, cw), 1))
                    st = jnp.where(causal, st, -jnp.inf)
                m_new, l_new, alpha, pv = _online_softmax_cols(st, vt, m_ref[hh, :, cols], l_ref[hh, :, cols])
                m_ref[hh, :, cols] = m_new
                l_ref[hh, :, cols] = l_new
                acc_ref[rows, cols] = alpha * acc_ref[rows, cols] + pv

    crosses_diagonal = j * tk + tk - 1 > i * tq

    @pl.when(crosses_diagonal)
    def _():
        step(True)

    @pl.when(jnp.logical_not(crosses_diagonal))
    def _():
        step(False)

    @pl.when((j + 1) * tk > i * tq + tq - 1)
    def _():
        outs = [acc_ref[hh * dh:(hh + 1) * dh, :] / l_ref[hh] for hh in range(n_sub)]
        o_ref[...] = jnp.concatenate(outs, axis=0).T.astype(o_ref.dtype)


def _fox_flash(q, qa, kb, ka, vtb, n_seq, dh, tq=1024, tk=1024):
    n, hd = q.shape
    s = n // n_seq
    tq = min(tq, s)
    tk = min(tk, s)
    lanes = FOX_LANES
    assert s % tq == 0 and s % tk == 0 and hd % lanes == 0 and lanes % dh == 0
    nq, nk = s // tq, s // tk
    n_sub = lanes // dh
    qi, kj = _causal_tile_pairs(s, tq, tk)
    q_spec = pl.BlockSpec((tq, lanes), lambda b, g, p, qi, kj: (b * nq + qi[p], g))
    k_spec = pl.BlockSpec((tk, lanes), lambda b, g, p, qi, kj: (b * nk + kj[p], g))
    grid_spec = pltpu.PrefetchScalarGridSpec(
        num_scalar_prefetch=2, grid=(n_seq, hd // lanes, qi.shape[0]),
        in_specs=[q_spec, q_spec, k_spec, k_spec,
                  pl.BlockSpec((1, lanes, tk), lambda b, g, p, qi, kj: (b, g, kj[p]))],
        out_specs=q_spec,
        scratch_shapes=[pltpu.VMEM((n_sub, 1, tq), F32), pltpu.VMEM((n_sub, 1, tq), F32),
                        pltpu.VMEM((lanes, tq), F32)])
    return pl.pallas_call(
        functools.partial(_fox_flash_kernel, tq=tq, tk=tk, dh=dh), name="fox_flash", grid_spec=grid_spec,
        out_shape=jax.ShapeDtypeStruct((n, hd), BF16),
        compiler_params=_cparams("parallel", "parallel", "arbitrary"))(qi, kj, q, qa, kb, ka, vtb)


def _fox_decode_kernel(pt_ref, q_ref, cs_ref, cst_ref, knew_ref, vnew_ref, suf_ref, *rest,
                       n_pp, page, n_heads, t_new):
    kt_refs = rest[:n_pp]
    vt_refs = rest[n_pp:2 * n_pp]
    lft_refs = rest[2 * n_pp:3 * n_pp]
    o_ref = rest[3 * n_pp]
    m_ref, l_ref, acc_ref, carry, qb_ref, kbuf, vbuf = rest[3 * n_pp + 1:]
    j = pl.program_id(1)
    hd = q_ref.shape[2]
    rows = t_new * n_heads
    dh = hd // n_heads

    @pl.when(j == 0)
    def _():
        _init_softmax_state(m_ref, l_ref, acc_ref)
        carry[...] = jnp.zeros(carry.shape, F32)
        own = (lax.broadcasted_iota(jnp.int32, (n_heads, hd), 1) // dh
               == lax.broadcasted_iota(jnp.int32, (n_heads, hd), 0))
        q = q_ref[0]
        qrep = jnp.where(own[None], jnp.broadcast_to(q[:, None, :], (t_new, n_heads, hd)), 0.0)
        qb_ref[...] = qrep.reshape(rows, hd).astype(BF16)

    qb = qb_ref[...]
    cs = cs_ref[0] * LOG2E

    parts = []
    for r in range(n_pp):
        parts += list(_split3(lft_refs[r][0, 0]))
    sums = _dot(jnp.concatenate(parts, axis=0), suf_ref[...])
    cur = carry[...]
    bias = []
    for r in range(n_pp):
        blk = sums[3 * n_heads * r:3 * n_heads * (r + 1), :]
        blk = blk[:n_heads] + blk[n_heads:2 * n_heads] + blk[2 * n_heads:]
        bias.append(blk[:, :page] + cur)
        cur = cur + blk[:, page:page + 1]
        kbuf[:, r * page:(r + 1) * page] = kt_refs[r][0, 0].reshape(hd, page).astype(BF16)
        vbuf[:, r * page:(r + 1) * page] = vt_refs[r][0, 0].reshape(hd, page).astype(BF16)
    carry[...] = cur
    after = jnp.concatenate(bias, axis=1) * LOG2E
    s = _dot(qb, kbuf[...]) + (cs + jnp.concatenate([after] * t_new, axis=0))
    m_prev = m_ref[...]
    m_new = jnp.maximum(m_prev, jnp.max(s, axis=1, keepdims=True))
    alpha = jnp.exp2(m_prev - m_new)
    p = jnp.exp2(s - m_new)
    l_ref[...] = alpha * l_ref[...] + jnp.sum(p, axis=1, keepdims=True)
    m_ref[...] = m_new
    acc_ref[...] = alpha * acc_ref[...] + _dot_nt(p.astype(BF16), vbuf[...])

    @pl.when(j == pl.num_programs(1) - 1)
    def _():
        kn = knew_ref[0]
        s2 = _dot_nt(qb, kn) + (cs - jnp.concatenate([cst_ref[0] * LOG2E] * t_new, axis=0))
        q_t = lax.broadcasted_iota(jnp.int32, (rows, 1), 0) // n_heads
        k_t = lax.broadcasted_iota(jnp.int32, (1, s2.shape[1]), 1)
        s2 = jnp.where(k_t <= q_t, s2, -jnp.inf)
        _online_softmax_step(s2, vnew_ref[0], m_ref, l_ref, acc_ref)
        o_ref[0] = _head_diag(acc_ref[...] / l_ref[...], n_heads, t_new).astype(o_ref.dtype)


def _fox_decode(q_s, cs_rows, cs_t, knew, vnew, cache_kt, cache_vt, cache_lft, layer, page_table):
    db, n_pages = page_table.shape
    n_heads, dh, page = cache_kt.shape[2:]
    hd = n_heads * dh
    t_new = q_s.shape[1]
    n_pp = min(FOX_PAGES_PER_STEP, n_pages)
    assert n_pages % n_pp == 0
    rows = t_new * n_heads
    per_b = lambda b, j, pt: (b, 0, 0)
    suf = jnp.concatenate([jnp.tril(jnp.ones((page, page), F32), -1),
                           jnp.ones((page, 1), F32), jnp.zeros((page, page - 1), F32)], axis=1).astype(BF16)

    def page_spec(r, shape):
        nd = len(shape)
        return pl.BlockSpec((1, 1) + shape,
                            lambda b, j, pt: (layer, pt[b, n_pages - 1 - (j * n_pp + r)]) + (0,) * nd)

    in_specs = ([pl.BlockSpec((1, t_new, hd), per_b), pl.BlockSpec((1, rows, 1), per_b),
                 pl.BlockSpec((1, n_heads, BF16_ROWS), per_b),
                 pl.BlockSpec((1, BF16_ROWS, hd), per_b), pl.BlockSpec((1, BF16_ROWS, hd), per_b),
                 _const_spec(suf.shape)]
                + [page_spec(r, (n_heads, dh, page)) for r in range(n_pp)] * 2
                + [page_spec(r, (n_heads, page)) for r in range(n_pp)])
    grid_spec = pltpu.PrefetchScalarGridSpec(
        num_scalar_prefetch=1, grid=(db, n_pages // n_pp), in_specs=in_specs,
        out_specs=pl.BlockSpec((1, t_new, hd), per_b),
        scratch_shapes=[pltpu.VMEM((rows, 1), F32), pltpu.VMEM((rows, 1), F32), pltpu.VMEM((rows, hd), F32),
                        pltpu.VMEM((n_heads, 1), F32), pltpu.VMEM((rows, hd), BF16),
                        pltpu.VMEM((hd, n_pp * page), BF16), pltpu.VMEM((hd, n_pp * page), BF16)])
    return pl.pallas_call(
        functools.partial(_fox_decode_kernel, n_pp=n_pp, page=page, n_heads=n_heads, t_new=t_new),
        name="fox_decode",
        grid_spec=grid_spec, out_shape=jax.ShapeDtypeStruct((db, t_new, hd), BF16),
        compiler_params=_cparams("parallel", "arbitrary"))(
            page_table, q_s, cs_rows, cs_t, knew, vnew, suf,
            *([cache_kt] * n_pp), *([cache_vt] * n_pp), *([cache_lft] * n_pp))


def _pad_rows(a, rows):
    return jnp.pad(a, ((0, 0), (0, rows - a.shape[1]), (0, 0)))


def _rope_tables(pos, dr, n_heads):
    inv = ROPE_THETA ** (-jnp.arange(0, dr, 2, dtype=F32) / dr)
    ang = pos.astype(F32)[:, None] * inv[None, :]
    return jnp.tile(jnp.cos(ang), (1, n_heads)), jnp.tile(jnp.sin(ang), (1, n_heads))


def kernel(x_prompt, x_sample, state_b_buf, cache_ckv_c, cache_kr_c, cache_k_d, cache_v_d, cache_logf_d, page_table, g_norm, w_up, w_down, w_in_a, g_v_a, b_v_a, w_s_a, b_s_a, w_out_a, w_in_b, w_grp_b, scale_b, w_in_c, g_q_c, g_kv_c, w_uq_c, w_uk_c, w_uv_c, w_o_c, w_in_d, b_f_d, w_o_d):
    B, S, D = x_prompt.shape
    DB, T, _ = x_sample.shape
    depth = g_norm.shape[0]
    n_pages = page_table.shape[1]
    page = cache_ckv_c.shape[2]
    past = n_pages * page
    pos_p = jnp.arange(S, dtype=jnp.int32)
    pos_s = past + jnp.arange(T, dtype=jnp.int32)
    bf = lambda w: w.astype(BF16)

    xp = x_prompt.reshape(B * S, D)
    xs = x_sample.reshape(DB * T, D)
    outs = {k: [] for k in ("v_a_s", "buf_b_p", "buf_b_s", "ckv_c_p", "kr_c_p", "ckv_c_s", "kr_c_s",
                            "k_d_p", "v_d_p", "logf_d_p", "k_d_s", "v_d_s", "logf_d_s")}
    for i in range(depth):
        kind, j = i % 4, i // 4
        g4 = g_norm[i]
        wu, wd = bf(w_up[i]), bf(w_down[i])
        if kind == 0:
            n_groups, chunk = w_s_a.shape[1], w_s_a.shape[2]
            d_a = w_out_a.shape[1]
            c_a = d_a // n_groups
            assert chunk % T == 0
            tril = jnp.tril(jnp.ones((chunk, chunk), bool))
            ws_p = bf(jnp.where(tril[None], w_s_a[j], 0.0))
            bias_p = jnp.repeat(b_s_a[j].T, c_a, axis=1)
            tril_t = jnp.tril(jnp.ones((T, T), bool))
            ws_t = jnp.where(tril_t[None], w_s_a[j][:, :T, :T], 0.0)
            eye = jnp.eye(chunk // T, dtype=F32)
            ws_s = bf(jnp.einsum("ab,gts->gatbs", eye, ws_t).reshape(n_groups, chunk, chunk))
            bias_s = jnp.tile(jnp.repeat(b_s_a[j][:, :T].T, c_a, axis=1), (chunk // T, 1))
            wa = (g4, bf(w_in_a[j]), g_v_a[j][None], b_v_a[j][None])
            xp, _ = _gmlp(xp, *wa, ws_p, bias_p, bf(w_out_a[j]), chunk)
            xs, v_new = _gmlp(xs, *wa, ws_s, bias_s, bf(w_out_a[j]), chunk)
            outs["v_a_s"].append(v_new.reshape(DB, T, d_a))
            xp = _ffn(xp, g4, wu, wd)
            xs = _ffn(xs, g4, wu, wd)
        elif kind == 1:
            wb = (g4, bf(w_in_b[j]), bf(w_grp_b[j]), scale_b[j][None])
            xp, buf_p = _pool_prompt(xp, *wb, n_seq=B)
            outs["buf_b_p"].append(buf_p[:, HIST_ROWS - POOL_HIST:, :])
            xs_tm = xs.reshape(DB, T, D).transpose(1, 0, 2).reshape(T * DB, D)
            hist_tm = state_b_buf[j].transpose(1, 0, 2).reshape(POOL_HIST * DB, D)
            xs_tm, z_tm = _pool_sample(xs_tm, hist_tm, *wb, db=DB, t_new=T, pos0=past)
            xs = xs_tm.reshape(T, DB, D).transpose(1, 0, 2).reshape(DB * T, D)
            z_s = z_tm.reshape(T, DB, D).transpose(1, 0, 2)
            outs["buf_b_s"].append(jnp.concatenate([state_b_buf[j], z_s], axis=1)[:, -POOL_HIST:])
            xp = _ffn(xp, g4, wu, wd)
            xs = _ffn(xs, g4, wu, wd)
        elif kind == 2:
            n_heads, kv_rank, dn = w_uk_c.shape[1:]
            dv = w_uv_c.shape[3]
            dr = cache_kr_c.shape[3]
            q_rank = g_q_c.shape[1]
            scale = float((dn + dr) ** -0.5) * LOG2E
            wuq3 = w_uq_c[j].reshape(q_rank, n_heads, dn + dr)
            wuq_perm = bf(jnp.concatenate([wuq3[:, :, :dn].reshape(q_rank, -1),
                                           wuq3[:, :, dn:dn + dr // 2].reshape(q_rank, -1),
                                           wuq3[:, :, dn + dr // 2:].reshape(q_rank, -1)], axis=1))
            wukt = bf(w_uk_c[j].transpose(0, 2, 1))
            wuvt = bf(w_uv_c[j].transpose(0, 2, 1))
            wuv_all = bf(w_uv_c[j].transpose(1, 0, 2).reshape(kv_rank, n_heads * dv))
            wc = (g4, bf(w_in_c[j]), g_q_c[j][None], g_kv_c[j][None], wuq_perm, wukt)
            cos_p, sin_p = _rope_tables(jnp.tile(pos_p, B), dr, n_heads)
            q_cat, kcat, ckvt, ckv, kr = _mla_proj(xp, *wc, cos_p, sin_p, scale)
            o_p = _mla_flash(q_cat, kcat, ckvt, wuvt, n_seq=B)
            outs["ckv_c_p"].append(ckv.reshape(B, S, kv_rank))
            outs["kr_c_p"].append(kr.reshape(B, S, dr))
            cos_s, sin_s = _rope_tables(jnp.tile(pos_s, DB), dr, n_heads)
            q_cat_s, kcat_s, _, ckv_s, kr_s = _mla_proj(xs, *wc, cos_s, sin_s, scale)
            dk = kv_rank + dr
            q_rows = q_cat_s.reshape(n_heads, DB, T, dk).transpose(1, 2, 0, 3).reshape(DB * T * n_heads, dk)
            knew = _pad_rows(kcat_s.reshape(DB, T, dk), BF16_ROWS)
            o_s = _mla_decode(q_rows, knew, wuv_all, cache_ckv_c, cache_kr_c.transpose(0, 1, 3, 2), j,
                              page_table, T, n_heads)
            outs["ckv_c_s"].append(ckv_s.reshape(DB, T, kv_rank))
            outs["kr_c_s"].append(kr_s.reshape(DB, T, dr))
            xp = _ffn(xp, g4, wu, wd, a=o_p, wo=bf(w_o_c[j]))
            xs = _ffn(xs, g4, wu, wd, a=o_s.reshape(DB * T, n_heads * dv), wo=bf(w_o_c[j]))
        else:
            n_heads, dh = cache_k_d.shape[3], cache_k_d.shape[4]
            hd = n_heads * dh
            scale = float(dh ** -0.5) * LOG2E
            wdd = (g4, bf(w_in_d[j]), b_f_d[j][None], scale)
            q, lf, _, kt, vt, kb, vtb, qa, ka = _fox_proj(xp, *wdd, n_seq=B)
            o_p = _fox_flash(q, qa, kb, ka, vtb, n_seq=B, dh=dh)
            outs["k_d_p"].append(kt.reshape(B, n_heads, dh, S).transpose(0, 3, 1, 2))
            outs["v_d_p"].append(vt.reshape(B, n_heads, dh, S).transpose(0, 3, 1, 2))
            outs["logf_d_p"].append(lf.reshape(B, S, n_heads))
            q_s, lf_s, cs, k_s, v_s, kb_s, vb_s = _fox_proj(xs, *wdd, seg=T)
            cs3 = cs.reshape(DB, T, n_heads)
            cs_t = jnp.pad(cs3.transpose(0, 2, 1), ((0, 0), (0, 0), (0, BF16_ROWS - T)))
            o_s = _fox_decode(q_s.astype(F32).reshape(DB, T, hd), cs3.reshape(DB, T * n_heads, 1), cs_t,
                              _pad_rows(kb_s.reshape(DB, T, hd), BF16_ROWS),
                              _pad_rows(vb_s.reshape(DB, T, hd), BF16_ROWS),
                              cache_k_d.transpose(0, 1, 3, 4, 2), cache_v_d.transpose(0, 1, 3, 4, 2),
                              cache_logf_d.transpose(0, 1, 3, 2), j, page_table)
            outs["k_d_s"].append(k_s.reshape(DB, T, n_heads, dh))
            outs["v_d_s"].append(v_s.reshape(DB, T, n_heads, dh))
            outs["logf_d_s"].append(lf_s.reshape(DB, T, n_heads))
            xp = _ffn(xp, g4, wu, wd, a=o_p, wo=bf(w_o_d[j]))
            xs = _ffn(xs, g4, wu, wd, a=o_s.reshape(DB * T, hd), wo=bf(w_o_d[j]))
    st = lambda name: jnp.stack(outs[name])
    return (xp.reshape(B, S, D), xs.reshape(DB, T, D), st("v_a_s"), st("buf_b_p"), st("buf_b_s"),
            st("ckv_c_p"), st("kr_c_p"), st("ckv_c_s"), st("kr_c_s"),
            st("k_d_p"), st("v_d_p"), st("logf_d_p"), st("k_d_s"), st("v_d_s"), st("logf_d_s"))
```

```python
import functools

import jax
import jax.numpy as jnp
import numpy as np
from jax import lax
from jax.experimental import pallas as pl
from jax.experimental.pallas import tpu as pltpu

F32 = jnp.float32
BF16 = jnp.bfloat16
EPS = 1e-6
LOG2E = 1.4426950408889634
ROPE_THETA = 10000.0
POOL_WINDOWS = (2, 4, 8, 16)
POOL_HIST = max(POOL_WINDOWS) - 1
HIST_ROWS = 16
BF16_ROWS = 16
V7X_VMEM_LIMIT = 56 * 1024 * 1024
FOX_PAGES_PER_STEP = 16
MLA_PAGES_PER_STEP = 32
MLA_CHUNK_COLS = 4096
FOX_CHUNK_COLS = 1024


def _cparams(*sem):
    return pltpu.CompilerParams(dimension_semantics=sem, vmem_limit_bytes=V7X_VMEM_LIMIT)


def _const_spec(shape):
    nd = len(shape)
    return pl.BlockSpec(shape, lambda *_: (0,) * nd, pipeline_mode=pl.Buffered(1))


def _rms(x, g):
    return x * lax.rsqrt(jnp.mean(x * x, axis=-1, keepdims=True) + EPS) * g


def _dot(a, b):
    return jnp.dot(a, b, preferred_element_type=F32)


def _dot_nt(a, b):
    return lax.dot_general(a, b, (((1,), (1,)), ((), ())), preferred_element_type=F32)


def _split3(x):
    hi = x.astype(BF16)
    r = x - hi.astype(F32)
    mid = r.astype(BF16)
    lo = (r - mid.astype(F32)).astype(BF16)
    return hi, mid, lo


def _cumsum_rows(x, seg=None):
    n = x.shape[0]
    r = lax.broadcasted_iota(jnp.int32, (n, n), 0)
    c = lax.broadcasted_iota(jnp.int32, (n, n), 1)
    keep = c <= r
    if seg is not None and seg < n:
        keep = keep & (c // seg == r // seg)
    tri = jnp.where(keep, 1.0, 0.0).astype(BF16)
    hi, mid, lo = _split3(x)
    return _dot(tri, hi) + _dot(tri, mid) + _dot(tri, lo)


def _online_softmax_cols(st, vt, m_prev, l_prev):
    m_new = jnp.maximum(m_prev, jnp.max(st, axis=0, keepdims=True))
    alpha = jnp.exp2(m_prev - m_new)
    p = jnp.exp2(st - m_new).astype(BF16)
    dv = vt.shape[0]
    pv = _dot(jnp.concatenate([vt, jnp.ones((BF16_ROWS, vt.shape[1]), BF16)], axis=0), p)
    l_new = alpha * l_prev + pv[dv:dv + 1, :]
    return m_new, l_new, alpha, pv[:dv, :]


def _online_softmax_step(s, v, m_ref, l_ref, acc_ref):
    m_prev = m_ref[...]
    m_new = jnp.maximum(m_prev, jnp.max(s, axis=1, keepdims=True))
    alpha = jnp.exp2(m_prev - m_new)
    p = jnp.exp2(s - m_new)
    l_ref[...] = alpha * l_ref[...] + jnp.sum(p, axis=1, keepdims=True)
    acc_ref[...] = alpha * acc_ref[...] + _dot(p.astype(BF16), v)
    m_ref[...] = m_new


def _init_softmax_state(m_ref, l_ref, acc_ref):
    m_ref[...] = jnp.full(m_ref.shape, -jnp.inf, F32)
    l_ref[...] = jnp.zeros(l_ref.shape, F32)
    acc_ref[...] = jnp.zeros(acc_ref.shape, F32)


def _ffn_kernel(*refs, fc, with_proj):
    if with_proj:
        a_ref, x_ref, wo_ref, g_ref, wu_ref, wd_ref, o_ref = refs
        x1 = x_ref[...] + _rms(_dot(a_ref[...], wo_ref[...]), g_ref[1:2, :])
    else:
        x_ref, g_ref, wu_ref, wd_ref, o_ref = refs
        x1 = x_ref[...]
    h = _rms(x1, g_ref[2:3, :]).astype(BF16)
    acc = None
    for c in range(wu_ref.shape[1] // fc):
        u = jnp.maximum(_dot(h, wu_ref[:, c * fc:(c + 1) * fc]), 0.0)
        d = _dot((u * u).astype(BF16), wd_ref[c * fc:(c + 1) * fc, :])
        acc = d if acc is None else acc + d
    o_ref[...] = x1 + _rms(acc, g_ref[3:4, :])


def _ffn(x, g4, wu, wd, a=None, wo=None, tm=512):
    n, d = x.shape
    tm = min(tm, n)
    assert n % tm == 0
    f = wu.shape[1]
    fc = min(1024, f)
    assert f % fc == 0
    row = lambda i: (i, 0)
    specs, args = [], []
    if a is not None:
        specs += [pl.BlockSpec((tm, a.shape[1]), row), pl.BlockSpec((tm, d), row), _const_spec(wo.shape)]
        args += [a, x, wo]
    else:
        specs += [pl.BlockSpec((tm, d), row)]
        args += [x]
    specs += [_const_spec(g4.shape), _const_spec(wu.shape), _const_spec(wd.shape)]
    args += [g4, wu, wd]
    return pl.pallas_call(
        functools.partial(_ffn_kernel, fc=fc, with_proj=a is not None), name="ffn",
        grid=(n // tm,), in_specs=specs, out_specs=pl.BlockSpec((tm, d), row),
        out_shape=jax.ShapeDtypeStruct((n, d), F32), compiler_params=_cparams("parallel"))(*args)


def _gmlp_kernel(x_ref, g_ref, win_ref, gv_ref, bv_ref, ws_ref, bias_ref, wout_ref, o_ref, v_ref, *, chunk):
    x = x_ref[...]
    d_a = wout_ref.shape[0]
    n_groups = ws_ref.shape[0]
    c_a = d_a // n_groups
    uv = jax.nn.gelu(_dot(_rms(x, g_ref[0:1, :]).astype(BF16), win_ref[...]))
    u = uv[:, :d_a]
    v = uv[:, d_a:]
    vc = v - jnp.mean(v, axis=-1, keepdims=True)
    v = vc * lax.rsqrt(jnp.mean(vc * vc, axis=-1, keepdims=True) + EPS) * gv_ref[...] + bv_ref[...]
    v_ref[...] = v
    vb = v.astype(BF16)
    rows = []
    for c in range(x.shape[0] // chunk):
        cols = [_dot(ws_ref[g], vb[c * chunk:(c + 1) * chunk, g * c_a:(g + 1) * c_a]) for g in range(n_groups)]
        rows.append(jnp.concatenate(cols, axis=1))
    mixed = jnp.concatenate(rows, axis=0) + bias_ref[...]
    m = _dot((u * mixed).astype(BF16), wout_ref[...])
    o_ref[...] = x + _rms(m, g_ref[1:2, :])


def _gmlp(x, g4, w_in, g_v, b_v, ws_eff, bias_rows, w_out, chunk, tm=256):
    n, d = x.shape
    tm = min(tm, n)
    assert n % tm == 0 and tm % chunk == 0
    d_a = w_out.shape[0]
    bias = jnp.tile(bias_rows, (tm // chunk, 1))
    row = lambda i: (i, 0)
    return pl.pallas_call(
        functools.partial(_gmlp_kernel, chunk=chunk), name="gmlp",
        grid=(n // tm,),
        in_specs=[pl.BlockSpec((tm, d), row), _const_spec(g4.shape), _const_spec(w_in.shape),
                  _const_spec(g_v.shape), _const_spec(b_v.shape), _const_spec(ws_eff.shape),
                  _const_spec(bias.shape), _const_spec(w_out.shape)],
        out_specs=[pl.BlockSpec((tm, d), row), pl.BlockSpec((tm, d_a), row)],
        out_shape=[jax.ShapeDtypeStruct((n, d), F32), jax.ShapeDtypeStruct((n, d_a), F32)],
        compiler_params=_cparams("parallel"))(x, g4, w_in, g_v, b_v, ws_eff, bias, w_out)


def _pool_finish(x, dmat, g_ref, wg_ref, sc_ref):
    n_g = wg_ref.shape[0]
    cb = x.shape[1] // n_g
    ys = [_dot(dmat[:, gi * cb:(gi + 1) * cb], wg_ref[gi]) for gi in range(n_g)]
    y = jnp.concatenate(ys, axis=1) * sc_ref[...]
    return x + _rms(y, g_ref[1:2, :])


def _pool_prompt_kernel(x_ref, g_ref, win_ref, wg_ref, sc_ref, o_ref, buf_ref, zbuf, *, tiles_per_seq):
    tm, d = x_ref.shape
    li = pl.program_id(0) % tiles_per_seq

    @pl.when(li == 0)
    def _():
        zbuf[0:HIST_ROWS, :] = jnp.zeros((HIST_ROWS, d), F32)

    x = x_ref[...]
    z = _dot(_rms(x, g_ref[0:1, :]).astype(BF16), win_ref[...])
    zbuf[HIST_ROWS:HIST_ROWS + tm, :] = z
    pos = li * tm + lax.broadcasted_iota(jnp.int32, (tm, 1), 0)
    cb = d // len(POOL_WINDOWS)
    cols = []
    for gi, w in enumerate(POOL_WINDOWS):
        sl = slice(gi * cb, (gi + 1) * cb)
        zg = z[:, sl]
        acc = zg
        for k in range(1, w):
            acc = acc + zbuf[HIST_ROWS - k:HIST_ROWS - k + tm, sl]
        cnt = jnp.minimum(w, pos + 1).astype(F32)
        cols.append(acc / cnt - zg)
    dmat = jnp.concatenate(cols, axis=1).astype(BF16)
    o_ref[...] = _pool_finish(x, dmat, g_ref, wg_ref, sc_ref)
    tail = z[tm - HIST_ROWS:, :]
    zbuf[0:HIST_ROWS, :] = tail
    buf_ref[0] = tail


def _pool_prompt(x, g4, w_in, w_grp, scale, n_seq, tm=256):
    n, d = x.shape
    s = n // n_seq
    tm = min(tm, s)
    assert s % tm == 0 and tm >= HIST_ROWS
    tps = s // tm
    row = lambda i: (i, 0)
    return pl.pallas_call(
        functools.partial(_pool_prompt_kernel, tiles_per_seq=tps), name="pool_prompt",
        grid=(n // tm,),
        in_specs=[pl.BlockSpec((tm, d), row), _const_spec(g4.shape), _const_spec(w_in.shape),
                  _const_spec(w_grp.shape), _const_spec(scale.shape)],
        out_specs=[pl.BlockSpec((tm, d), row), pl.BlockSpec((1, HIST_ROWS, d), lambda i: (i // tps, 0, 0))],
        out_shape=[jax.ShapeDtypeStruct((n, d), F32), jax.ShapeDtypeStruct((n_seq, HIST_ROWS, d), F32)],
        scratch_shapes=[pltpu.VMEM((HIST_ROWS + tm, d), F32)],
        compiler_params=_cparams("arbitrary"))(x, g4, w_in, w_grp, scale)


def _pool_sample_kernel(x_ref, hist_ref, g_ref, win_ref, wg_ref, sc_ref, o_ref, z_ref, *, db, t_new, pos0):
    x = x_ref[...]
    d = x.shape[1]
    z = _dot(_rms(x, g_ref[0:1, :]).astype(BF16), win_ref[...])
    z_ref[...] = z
    cb = d // len(POOL_WINDOWS)

    def rows_at(p, sl):
        if p >= POOL_HIST:
            return z[(p - POOL_HIST) * db:(p - POOL_HIST + 1) * db, sl]
        return hist_ref[p * db:(p + 1) * db, sl]

    drows = []
    for t in range(t_new):
        cols = []
        for gi, w in enumerate(POOL_WINDOWS):
            sl = slice(gi * cb, (gi + 1) * cb)
            acc = rows_at(POOL_HIST + t, sl)
            for k in range(1, w):
                acc = acc + rows_at(POOL_HIST + t - k, sl)
            cnt = float(min(w, pos0 + t + 1))
            cols.append(acc / cnt - z[t * db:(t + 1) * db, sl])
        drows.append(jnp.concatenate(cols, axis=1))
    dmat = jnp.concatenate(drows, axis=0).astype(BF16)
    o_ref[...] = _pool_finish(x, dmat, g_ref, wg_ref, sc_ref)


def _pool_sample(x_tm, hist_tm, g4, w_in, w_grp, scale, db, t_new, pos0):
    n, d = x_tm.shape
    args = (x_tm, hist_tm, g4, w_in, w_grp, scale)
    return pl.pallas_call(
        functools.partial(_pool_sample_kernel, db=db, t_new=t_new, pos0=pos0), name="pool_sample",
        grid=(1,),
        in_specs=[_const_spec(a.shape) for a in args],
        out_specs=[pl.BlockSpec((n, d), lambda i: (0, 0)), pl.BlockSpec((n, d), lambda i: (0, 0))],
        out_shape=[jax.ShapeDtypeStruct((n, d), F32), jax.ShapeDtypeStruct((n, d), F32)],
        compiler_params=_cparams("arbitrary"))(*args)


def _mla_proj_kernel(x_ref, g_ref, win_ref, gq_ref, gkv_ref, wuq_ref, wukt_ref, cos_ref, sin_ref,
                     q_ref, kcat_ref, ckvt_ref, ckv_ref, kr_ref, *, q_rank, kv_rank, scale):
    n_heads, dn, _ = wukt_ref.shape
    x = x_ref[...]
    c = _dot(_rms(x, g_ref[0:1, :]).astype(BF16), win_ref[...])
    c_q = c[:, :q_rank]
    c_kv = c[:, q_rank:q_rank + kv_rank]
    k_r = c[:, q_rank + kv_rank:]
    q = _dot(_rms(c_q, gq_ref[...]).astype(BF16), wuq_ref[...])
    hd = n_heads * dn
    half = (q.shape[1] - hd) // 2
    hr = half // n_heads
    x1 = q[:, hd:hd + half]
    x2 = q[:, hd + half:]
    cosq = cos_ref[...]
    sinq = sin_ref[...]
    r1 = ((x1 * cosq - x2 * sinq) * scale).astype(BF16)
    r2 = ((x1 * sinq + x2 * cosq) * scale).astype(BF16)
    ckv = _rms(c_kv, gkv_ref[...])
    ckv_ref[...] = ckv
    ckvt_ref[...] = ckv.T.astype(BF16)
    k1 = k_r[:, :hr]
    k2 = k_r[:, hr:]
    c1 = cosq[:, :hr]
    s1 = sinq[:, :hr]
    kr = jnp.concatenate([k1 * c1 - k2 * s1, k1 * s1 + k2 * c1], axis=1)
    kr_ref[...] = kr
    kcat_ref[:, :kv_rank] = ckv.astype(BF16)
    kcat_ref[:, kv_rank:] = kr.astype(BF16)
    for h in range(n_heads):
        ql = _dot(q[:, h * dn:(h + 1) * dn].astype(BF16), wukt_ref[h]) * scale
        q_ref[h, :, :kv_rank] = ql.astype(BF16)
        q_ref[h, :, kv_rank:kv_rank + hr] = r1[:, h * hr:(h + 1) * hr]
        q_ref[h, :, kv_rank + hr:] = r2[:, h * hr:(h + 1) * hr]


def _mla_proj(x, g4, w_in, g_q, g_kv, wuq_perm, wukt, cosq, sinq, scale, tm=256):
    n, d = x.shape
    tm = min(tm, n)
    assert n % tm == 0
    n_heads, _, kv_rank = wukt.shape
    q_rank = g_q.shape[1]
    dr = w_in.shape[1] - q_rank - kv_rank
    dk = kv_rank + dr
    row = lambda i: (i, 0)
    return pl.pallas_call(
        functools.partial(_mla_proj_kernel, q_rank=q_rank, kv_rank=kv_rank, scale=scale), name="mla_proj",
        grid=(n // tm,),
        in_specs=[pl.BlockSpec((tm, d), row), _const_spec(g4.shape), _const_spec(w_in.shape),
                  _const_spec(g_q.shape), _const_spec(g_kv.shape), _const_spec(wuq_perm.shape),
                  _const_spec(wukt.shape), pl.BlockSpec((tm, cosq.shape[1]), row),
                  pl.BlockSpec((tm, sinq.shape[1]), row)],
        out_specs=[pl.BlockSpec((n_heads, tm, dk), lambda i: (0, i, 0)), pl.BlockSpec((tm, dk), row),
                   pl.BlockSpec((kv_rank, tm), lambda i: (0, i)),
                   pl.BlockSpec((tm, kv_rank), row), pl.BlockSpec((tm, dr), row)],
        out_shape=[jax.ShapeDtypeStruct((n_heads, n, dk), BF16), jax.ShapeDtypeStruct((n, dk), BF16),
                   jax.ShapeDtypeStruct((kv_rank, n), BF16),
                   jax.ShapeDtypeStruct((n, kv_rank), F32), jax.ShapeDtypeStruct((n, dr), F32)],
        compiler_params=_cparams("parallel"))(x, g4, w_in, g_q, g_kv, wuq_perm, wukt, cosq, sinq)


def _causal_tile_pairs(s, tq, tk):
    pairs = [(i, j) for i in range(s // tq) for j in range((i * tq + tq - 1) // tk + 1)]
    return (jnp.asarray([p[0] for p in pairs], jnp.int32), jnp.asarray([p[1] for p in pairs], jnp.int32))


def _mla_flash_kernel(qi_ref, kj_ref, q_ref, k_ref, vt_ref, wuvt_ref, o_ref, m_ref, l_ref, acc_ref, *, tq, tk):
    p = pl.program_id(1)
    i = qi_ref[p]
    j = kj_ref[p]
    n_heads = q_ref.shape[0]
    m_cols = n_heads * tq

    @pl.when(j == 0)
    def _():
        _init_softmax_state(m_ref, l_ref, acc_ref)

    def step(masked):
        k = k_ref[...]
        vt = vt_ref[...]
        hpc = max(1, min(n_heads, MLA_CHUNK_COLS // tq))
        cw = hpc * tq
        if masked:
            q_pos = i * tq + (lax.broadcasted_iota(jnp.int32, (1, cw), 1) & (tq - 1))
            causal = j * tk + lax.broadcasted_iota(jnp.int32, (tk, 1), 0) <= q_pos
        for c in range(n_heads // hpc):
            cols = slice(c * cw, (c + 1) * cw)
            st = _dot_nt(k, q_ref[c * hpc:(c + 1) * hpc].reshape(cw, q_ref.shape[2]))
            if masked:
                st = jnp.where(causal, st, -jnp.inf)
            m_new, l_new, alpha, pv = _online_softmax_cols(st, vt, m_ref[:, cols], l_ref[:, cols])
            m_ref[:, cols] = m_new
            l_ref[:, cols] = l_new
            acc_ref[:, cols] = alpha * acc_ref[:, cols] + pv

    crosses_diagonal = j * tk + tk - 1 > i * tq

    @pl.when(crosses_diagonal)
    def _():
        step(True)

    @pl.when(jnp.logical_not(crosses_diagonal))
    def _():
        step(False)

    @pl.when((j + 1) * tk > i * tq + tq - 1)
    def _():
        ot = (acc_ref[...] / l_ref[...]).astype(BF16)
        outs = [_dot(wuvt_ref[h], ot[:, h * tq:(h + 1) * tq]) for h in range(n_heads)]
        o_ref[...] = jnp.concatenate(outs, axis=0).T.astype(o_ref.dtype)


def _mla_flash(q_cat, kcat, ckvt, wuvt, n_seq, tq=256, tk=512):
    n_heads, n, dk = q_cat.shape
    s = n // n_seq
    tq = min(tq, s)
    tk = min(tk, s)
    assert s % tq == 0 and s % tk == 0 and tq & (tq - 1) == 0
    nq, nk = s // tq, s // tk
    dv, kv_rank = wuvt.shape[1], wuvt.shape[2]
    qi, kj = _causal_tile_pairs(s, tq, tk)
    grid_spec = pltpu.PrefetchScalarGridSpec(
        num_scalar_prefetch=2, grid=(n_seq, qi.shape[0]),
        in_specs=[pl.BlockSpec((n_heads, tq, dk), lambda b, p, qi, kj: (0, b * nq + qi[p], 0)),
                  pl.BlockSpec((tk, dk), lambda b, p, qi, kj: (b * nk + kj[p], 0)),
                  pl.BlockSpec((kv_rank, tk), lambda b, p, qi, kj: (0, b * nk + kj[p])),
                  _const_spec(wuvt.shape)],
        out_specs=pl.BlockSpec((tq, n_heads * dv), lambda b, p, qi, kj: (b * nq + qi[p], 0)),
        scratch_shapes=[pltpu.VMEM((1, n_heads * tq), F32), pltpu.VMEM((1, n_heads * tq), F32),
                        pltpu.VMEM((kv_rank, n_heads * tq), F32)])
    return pl.pallas_call(
        functools.partial(_mla_flash_kernel, tq=tq, tk=tk), name="mla_flash", grid_spec=grid_spec,
        out_shape=jax.ShapeDtypeStruct((n, n_heads * dv), BF16),
        compiler_params=_cparams("parallel", "arbitrary"))(qi, kj, q_cat, kcat, ckvt, wuvt)


def _head_diag(full, n_heads, t_new):
    width = full.shape[1]
    dv = width // n_heads
    own = (lax.broadcasted_iota(jnp.int32, (n_heads, width), 1) // dv
           == lax.broadcasted_iota(jnp.int32, (n_heads, width), 0))
    full3 = full.reshape(t_new, n_heads, width)
    return jnp.sum(jnp.where(own[None], full3, 0.0), axis=1)


def _mla_decode_kernel(pt_ref, q_ref, knew_ref, wuv_ref, *rest, n_b, n_pp, page, kv_rank, t_new, n_heads):
    n_pg = n_b * n_pp
    ckv_refs = rest[:n_pg]
    krt_refs = rest[n_pg:2 * n_pg]
    o_ref = rest[2 * n_pg]
    scratch = rest[2 * n_pg + 1:]
    state = [scratch[5 * u:5 * u + 5] for u in range(n_b)]
    j = pl.program_id(1)
    rows = t_new * n_heads

    @pl.when(j == 0)
    def _():
        for m_ref, l_ref, acc_ref, _, _ in state:
            _init_softmax_state(m_ref, l_ref, acc_ref)

    for u, (m_ref, l_ref, acc_ref, cbuf, rbuf) in enumerate(state):
        for r in range(n_pp):
            cbuf[r * page:(r + 1) * page, :] = ckv_refs[u * n_pp + r][0, 0].astype(BF16)
            rbuf[:, r * page:(r + 1) * page] = krt_refs[u * n_pp + r][0, 0].astype(BF16)
        q = q_ref[u * rows:(u + 1) * rows, :]
        cb = cbuf[...]
        s = _dot_nt(q[:, :kv_rank], cb) + _dot(q[:, kv_rank:], rbuf[...])
        _online_softmax_step(s, cb, m_ref, l_ref, acc_ref)

    @pl.when(j == pl.num_programs(1) - 1)
    def _():
        for u, (m_ref, l_ref, acc_ref, _, _) in enumerate(state):
            q = q_ref[u * rows:(u + 1) * rows, :]
            kn = knew_ref[u]
            s = _dot_nt(q, kn)
            q_t = lax.broadcasted_iota(jnp.int32, (s.shape[0], 1), 0) // n_heads
            k_t = lax.broadcasted_iota(jnp.int32, (1, s.shape[1]), 1)
            s = jnp.where(k_t <= q_t, s, -jnp.inf)
            _online_softmax_step(s, kn[:, :kv_rank], m_ref, l_ref, acc_ref)
            ol = (acc_ref[...] / l_ref[...]).astype(BF16)
            o_ref[u] = _head_diag(_dot(ol, wuv_ref[...]), n_heads, t_new).astype(o_ref.dtype)


def _mla_decode(q_rows, knew, wuv_all, cache_ckv, cache_krt, layer, page_table, t_new, n_heads, n_b=2):
    db, n_pages = page_table.shape
    page, kv_rank = cache_ckv.shape[2], cache_ckv.shape[3]
    dr = cache_krt.shape[2]
    dk = kv_rank + dr
    n_pp = min(MLA_PAGES_PER_STEP, n_pages)
    assert n_pages % n_pp == 0 and db % n_b == 0
    rows = t_new * n_heads
    width = wuv_all.shape[1]

    def page_spec(u, r, shape):
        return pl.BlockSpec((1, 1) + shape, lambda b, j, pt: (layer, pt[b * n_b + u, j * n_pp + r], 0, 0))

    in_specs = ([pl.BlockSpec((n_b * rows, dk), lambda b, j, pt: (b, 0)),
                 pl.BlockSpec((n_b, BF16_ROWS, dk), lambda b, j, pt: (b, 0, 0)),
                 _const_spec(wuv_all.shape)]
                + [page_spec(u, r, (page, kv_rank)) for u in range(n_b) for r in range(n_pp)]
                + [page_spec(u, r, (dr, page)) for u in range(n_b) for r in range(n_pp)])
    per_sample = [pltpu.VMEM((rows, 1), F32), pltpu.VMEM((rows, 1), F32), pltpu.VMEM((rows, kv_rank), F32),
                  pltpu.VMEM((n_pp * page, kv_rank), BF16), pltpu.VMEM((dr, n_pp * page), BF16)]
    grid_spec = pltpu.PrefetchScalarGridSpec(
        num_scalar_prefetch=1, grid=(db // n_b, n_pages // n_pp), in_specs=in_specs,
        out_specs=pl.BlockSpec((n_b, t_new, width), lambda b, j, pt: (b, 0, 0)),
        scratch_shapes=per_sample * n_b)
    return pl.pallas_call(
        functools.partial(_mla_decode_kernel, n_b=n_b, n_pp=n_pp, page=page, kv_rank=kv_rank, t_new=t_new,
                          n_heads=n_heads),
        name="mla_decode",
        grid_spec=grid_spec, out_shape=jax.ShapeDtypeStruct((db, t_new, width), BF16),
        compiler_params=_cparams("parallel", "arbitrary"))(
            page_table, q_rows, knew, wuv_all, *([cache_ckv] * (n_b * n_pp)), *([cache_krt] * (n_b * n_pp)))


def _fox_proj_kernel(x_ref, g_ref, win_ref, bf_ref, place_ref, ones_ref, q_ref, lf_ref, c_ref, *rest,
                     hd, scale, seg, tiles_per_seq):
    x = x_ref[...]
    y = _dot(_rms(x, g_ref[0:1, :]).astype(BF16), win_ref[...])
    q_ref[...] = (y[:, :hd] * scale).astype(BF16)
    k = y[:, hd:2 * hd]
    v = y[:, 2 * hd:3 * hd]
    lf = jax.nn.log_sigmoid(y[:, 3 * hd:] + bf_ref[...])
    lf_ref[...] = lf
    c = _cumsum_rows(lf, seg)
    if tiles_per_seq is None:
        k_ref, v_ref, kb_ref, vb_ref = rest
        k_ref[...] = k
        v_ref[...] = v
        kb_ref[...] = k.astype(BF16)
        vb_ref[...] = v.astype(BF16)
    else:
        kt_ref, vt_ref, kb_ref, vtb_ref, qa_ref, ka_ref, carry = rest

        @pl.when(pl.program_id(0) % tiles_per_seq == 0)
        def _():
            carry[...] = jnp.zeros(carry.shape, F32)

        c = c + carry[...]
        carry[...] = c[c.shape[0] - 1:, :]
        vt = v.T
        kt_ref[0] = k.T
        vt_ref[0] = vt
        kb_ref[...] = k.astype(BF16)
        vtb_ref[0] = vt.astype(BF16)
        n_heads = c.shape[1]
        parts = _split3(c * LOG2E)
        for side, out_ref in ((0, qa_ref), (1, ka_ref)):
            aug = ones_ref[side:side + 1, :]
            for t, part in enumerate(parts):
                aug = aug + _dot(part, place_ref[side, t * n_heads:(t + 1) * n_heads, :])
            out_ref[...] = aug.astype(BF16)
    c_ref[...] = c


def _fox_proj(x, g4, w_in, b_f, scale, n_seq=None, seg=None, tm=256):
    n, d = x.shape
    n_heads = b_f.shape[1]
    hd = (w_in.shape[1] - n_heads) // 3
    s = n // n_seq if n_seq else n
    tm = min(tm, s)
    assert s % tm == 0 and (n_seq or tm % seg == 0)
    tps = s // tm
    row = lambda i: (i, 0)
    wide = pl.BlockSpec((tm, hd), row)
    narrow = pl.BlockSpec((tm, n_heads), row)
    out_specs = [wide, narrow, narrow]
    out_shape = [jax.ShapeDtypeStruct((n, hd), BF16), jax.ShapeDtypeStruct((n, n_heads), F32),
                 jax.ShapeDtypeStruct((n, n_heads), F32)]
    scratch = []
    if n_seq:
        tspec = pl.BlockSpec((1, hd, tm), lambda i: (i // tps, 0, i % tps))
        out_specs += [tspec, tspec, wide, tspec, wide, wide]
        out_shape += [jax.ShapeDtypeStruct((n_seq, hd, s), F32), jax.ShapeDtypeStruct((n_seq, hd, s), F32),
                      jax.ShapeDtypeStruct((n, hd), BF16), jax.ShapeDtypeStruct((n_seq, hd, s), BF16),
                      jax.ShapeDtypeStruct((n, hd), BF16), jax.ShapeDtypeStruct((n, hd), BF16)]
        scratch.append(pltpu.VMEM((1, n_heads), F32))
    else:
        out_specs += [wide, wide, wide, wide]
        out_shape += [jax.ShapeDtypeStruct((n, hd), F32), jax.ShapeDtypeStruct((n, hd), F32),
                      jax.ShapeDtypeStruct((n, hd), BF16), jax.ShapeDtypeStruct((n, hd), BF16)]
    place, ones = _fox_aug_tables(n_heads, hd // n_heads)
    return pl.pallas_call(
        functools.partial(_fox_proj_kernel, hd=hd, scale=scale, seg=seg, tiles_per_seq=tps if n_seq else None),
        name="fox_proj",
        grid=(n // tm,),
        in_specs=[pl.BlockSpec((tm, d), row), _const_spec(g4.shape), _const_spec(w_in.shape),
                  _const_spec(b_f.shape), _const_spec(place.shape), _const_spec(ones.shape)],
        out_specs=out_specs, out_shape=out_shape, scratch_shapes=scratch,
        compiler_params=_cparams("arbitrary"))(x, g4, w_in, b_f, place, ones)


FOX_LANES = 128
FOX_AUG_STRIDE = 8


def _fox_aug_tables(n_heads, dh):
    n_sub = FOX_LANES // dh
    place = np.zeros((2, 3 * n_heads, n_heads * dh), np.float32)
    ones = np.zeros((2, n_heads * dh), np.float32)
    for head in range(n_heads):
        base = (head // n_sub) * FOX_LANES + (head % n_sub) * FOX_AUG_STRIDE
        for t in range(3):
            place[0, t * n_heads + head, base + 3 + t] = 1.0
            place[1, t * n_heads + head, base + t] = -1.0
            ones[0, base + t] = 1.0
            ones[1, base + 3 + t] = 1.0
    return jnp.asarray(place, BF16), jnp.asarray(ones, F32)


def _fox_flash_kernel(qi_ref, kj_ref, q_ref, qa_ref, k_ref, ka_ref, vt_ref, o_ref, m_ref, l_ref, acc_ref,
                      *, tq, tk, dh):
    p = pl.program_id(2)
    i = qi_ref[p]
    j = kj_ref[p]
    width = q_ref.shape[1]
    n_sub = width // dh
    lane = lax.broadcasted_iota(jnp.int32, (1, 2 * width), 1)

    @pl.when(j == 0)
    def _():
        _init_softmax_state(m_ref, l_ref, acc_ref)

    def step(masked):
        k = jnp.concatenate([k_ref[...], ka_ref[...]], axis=1)
        cw = min(tq, FOX_CHUNK_COLS)
        for hh in range(n_sub):
            own = ((lane // dh == hh)
                   | ((lane >= width + hh * FOX_AUG_STRIDE) & (lane < width + (hh + 1) * FOX_AUG_STRIDE)))
            km = jnp.where(own, k, jnp.zeros_like(k))
            rows = slice(hh * dh, (hh + 1) * dh)
            vt = vt_ref[0, rows, :]
            for c in range(tq // cw):
                cols = slice(c * cw, (c + 1) * cw)
                st = _dot_nt(km, jnp.concatenate([q_ref[cols, :], qa_ref[cols, :]], axis=1))
                if masked:
                    causal = (j * tk + lax.broadcasted_iota(jnp.int32, (tk, 1), 0)
                              <= i * tq + c * cw + lax.broadcasted_iota(jnp.int32, (1, cw), 1))
                    st = jnp.where(causal, st, -jnp.inf)
                m_new, l_new, alpha, pv = _online_softmax_cols(st, vt, m_ref[hh, :, cols], l_ref[hh, :, cols])
                m_ref[hh, :, cols] = m_new
                l_ref[hh, :, cols] = l_new
                acc_ref[rows, cols] = alpha * acc_ref[rows, cols] + pv

    crosses_diagonal = j * tk + tk - 1 > i * tq

    @pl.when(crosses_diagonal)
    def _():
        step(True)

    @pl.when(jnp.logical_not(crosses_diagonal))
    def _():
        step(False)

    @pl.when((j + 1) * tk > i * tq + tq - 1)
    def _():
        outs = [acc_ref[hh * dh:(hh + 1) * dh, :] / l_ref[hh] for hh in range(n_sub)]
        o_ref[...] = jnp.concatenate(outs, axis=0).T.astype(o_ref.dtype)


def _fox_flash(q, qa, kb, ka, vtb, n_seq, dh, tq=1024, tk=1024):
    n, hd = q.shape
    s = n // n_seq
    tq = min(tq, s)
    tk = min(tk, s)
    lanes = FOX_LANES
    assert s % tq == 0 and s % tk == 0 and hd % lanes == 0 and lanes % dh == 0
    nq, nk = s // tq, s // tk
    n_sub = lanes // dh
    qi, kj = _causal_tile_pairs(s, tq, tk)
    q_spec = pl.BlockSpec((tq, lanes), lambda b, g, p, qi, kj: (b * nq + qi[p], g))
    k_spec = pl.BlockSpec((tk, lanes), lambda b, g, p, qi, kj: (b * nk + kj[p], g))
    grid_spec = pltpu.PrefetchScalarGridSpec(
        num_scalar_prefetch=2, grid=(n_seq, hd // lanes, qi.shape[0]),
        in_specs=[q_spec, q_spec, k_spec, k_spec,
                  pl.BlockSpec((1, lanes, tk), lambda b, g, p, qi, kj: (b, g, kj[p]))],
        out_specs=q_spec,
        scratch_shapes=[pltpu.VMEM((n_sub, 1, tq), F32), pltpu.VMEM((n_sub, 1, tq), F32),
                        pltpu.VMEM((lanes, tq), F32)])
    return pl.pallas_call(
        functools.partial(_fox_flash_kernel, tq=tq, tk=tk, dh=dh), name="fox_flash", grid_spec=grid_spec,
        out_shape=jax.ShapeDtypeStruct((n, hd), BF16),
        compiler_params=_cparams("parallel", "parallel", "arbitrary"))(qi, kj, q, qa, kb, ka, vtb)


def _fox_decode_kernel(pt_ref, q_ref, cs_ref, cst_ref, knew_ref, vnew_ref, suf_ref, *rest,
                       n_pp, page, n_heads, t_new):
    kt_refs = rest[:n_pp]
    vt_refs = rest[n_pp:2 * n_pp]
    lft_refs = rest[2 * n_pp:3 * n_pp]
    o_ref = rest[3 * n_pp]
    m_ref, l_ref, acc_ref, carry, qb_ref, kbuf, vbuf = rest[3 * n_pp + 1:]
    j = pl.program_id(1)
    hd = q_ref.shape[2]
    rows = t_new * n_heads
    dh = hd // n_heads

    @pl.when(j == 0)
    def _():
        _init_softmax_state(m_ref, l_ref, acc_ref)
        carry[...] = jnp.zeros(carry.shape, F32)
        own = (lax.broadcasted_iota(jnp.int32, (n_heads, hd), 1) // dh
               == lax.broadcasted_iota(jnp.int32, (n_heads, hd), 0))
        q = q_ref[0]
        qrep = jnp.where(own[None], jnp.broadcast_to(q[:, None, :], (t_new, n_heads, hd)), 0.0)
        qb_ref[...] = qrep.reshape(rows, hd).astype(BF16)

    qb = qb_ref[...]
    cs = cs_ref[0] * LOG2E

    parts = []
    for r in range(n_pp):
        parts += list(_split3(lft_refs[r][0, 0]))
    sums = _dot(jnp.concatenate(parts, axis=0), suf_ref[...])
    cur = carry[...]
    bias = []
    for r in range(n_pp):
        blk = sums[3 * n_heads * r:3 * n_heads * (r + 1), :]
        blk = blk[:n_heads] + blk[n_heads:2 * n_heads] + blk[2 * n_heads:]
        bias.append(blk[:, :page] + cur)
        cur = cur + blk[:, page:page + 1]
        kbuf[:, r * page:(r + 1) * page] = kt_refs[r][0, 0].reshape(hd, page).astype(BF16)
        vbuf[:, r * page:(r + 1) * page] = vt_refs[r][0, 0].reshape(hd, page).astype(BF16)
    carry[...] = cur
    after = jnp.concatenate(bias, axis=1) * LOG2E
    s = _dot(qb, kbuf[...]) + (cs + jnp.concatenate([after] * t_new, axis=0))
    m_prev = m_ref[...]
    m_new = jnp.maximum(m_prev, jnp.max(s, axis=1, keepdims=True))
    alpha = jnp.exp2(m_prev - m_new)
    p = jnp.exp2(s - m_new)
    l_ref[...] = alpha * l_ref[...] + jnp.sum(p, axis=1, keepdims=True)
    m_ref[...] = m_new
    acc_ref[...] = alpha * acc_ref[...] + _dot_nt(p.astype(BF16), vbuf[...])

    @pl.when(j == pl.num_programs(1) - 1)
    def _():
        kn = knew_ref[0]
        s2 = _dot_nt(qb, kn) + (cs - jnp.concatenate([cst_ref[0] * LOG2E] * t_new, axis=0))
        q_t = lax.broadcasted_iota(jnp.int32, (rows, 1), 0) // n_heads
        k_t = lax.broadcasted_iota(jnp.int32, (1, s2.shape[1]), 1)
        s2 = jnp.where(k_t <= q_t, s2, -jnp.inf)
        _online_softmax_step(s2, vnew_ref[0], m_ref, l_ref, acc_ref)
        o_ref[0] = _head_diag(acc_ref[...] / l_ref[...], n_heads, t_new).astype(o_ref.dtype)


def _fox_decode(q_s, cs_rows, cs_t, knew, vnew, cache_kt, cache_vt, cache_lft, layer, page_table):
    db, n_pages = page_table.shape
    n_heads, dh, page = cache_kt.shape[2:]
    hd = n_heads * dh
    t_new = q_s.shape[1]
    n_pp = min(FOX_PAGES_PER_STEP, n_pages)
    assert n_pages % n_pp == 0
    rows = t_new * n_heads
    per_b = lambda b, j, pt: (b, 0, 0)
    suf = jnp.concatenate([jnp.tril(jnp.ones((page, page), F32), -1),
                           jnp.ones((page, 1), F32), jnp.zeros((page, page - 1), F32)], axis=1).astype(BF16)

    def page_spec(r, shape):
        nd = len(shape)
        return pl.BlockSpec((1, 1) + shape,
                            lambda b, j, pt: (layer, pt[b, n_pages - 1 - (j * n_pp + r)]) + (0,) * nd)

    in_specs = ([pl.BlockSpec((1, t_new, hd), per_b), pl.BlockSpec((1, rows, 1), per_b),
                 pl.BlockSpec((1, n_heads, BF16_ROWS), per_b),
                 pl.BlockSpec((1, BF16_ROWS, hd), per_b), pl.BlockSpec((1, BF16_ROWS, hd), per_b),
                 _const_spec(suf.shape)]
                + [page_spec(r, (n_heads, dh, page)) for r in range(n_pp)] * 2
                + [page_spec(r, (n_heads, page)) for r in range(n_pp)])
    grid_spec = pltpu.PrefetchScalarGridSpec(
        num_scalar_prefetch=1, grid=(db, n_pages // n_pp), in_specs=in_specs,
        out_specs=pl.BlockSpec((1, t_new, hd), per_b),
        scratch_shapes=[pltpu.VMEM((rows, 1), F32), pltpu.VMEM((rows, 1), F32), pltpu.VMEM((rows, hd), F32),
                        pltpu.VMEM((n_heads, 1), F32), pltpu.VMEM((rows, hd), BF16),
                        pltpu.VMEM((hd, n_pp * page), BF16), pltpu.VMEM((hd, n_pp * page), BF16)])
    return pl.pallas_call(
        functools.partial(_fox_decode_kernel, n_pp=n_pp, page=page, n_heads=n_heads, t_new=t_new),
        name="fox_decode",
        grid_spec=grid_spec, out_shape=jax.ShapeDtypeStruct((db, t_new, hd), BF16),
        compiler_params=_cparams("parallel", "arbitrary"))(
            page_table, q_s, cs_rows, cs_t, knew, vnew, suf,
            *([cache_kt] * n_pp), *([cache_vt] * n_pp), *([cache_lft] * n_pp))


def _pad_rows(a, rows):
    return jnp.pad(a, ((0, 0), (0, rows - a.shape[1]), (0, 0)))


def _rope_tables(pos, dr, n_heads):
    inv = ROPE_THETA ** (-jnp.arange(0, dr, 2, dtype=F32) / dr)
    ang = pos.astype(F32)[:, None] * inv[None, :]
    return jnp.tile(jnp.cos(ang), (1, n_heads)), jnp.tile(jnp.sin(ang), (1, n_heads))


def kernel(x_prompt, x_sample, state_b_buf, cache_ckv_c, cache_kr_c, cache_k_d, cache_v_d, cache_logf_d, page_table, g_norm, w_up, w_down, w_in_a, g_v_a, b_v_a, w_s_a, b_s_a, w_out_a, w_in_b, w_grp_b, scale_b, w_in_c, g_q_c, g_kv_c, w_uq_c, w_uk_c, w_uv_c, w_o_c, w_in_d, b_f_d, w_o_d):
    B, S, D = x_prompt.shape
    DB, T, _ = x_sample.shape
    depth = g_norm.shape[0]
    n_pages = page_table.shape[1]
    page = cache_ckv_c.shape[2]
    past = n_pages * page
    pos_p = jnp.arange(S, dtype=jnp.int32)
    pos_s = past + jnp.arange(T, dtype=jnp.int32)
    bf = lambda w: w.astype(BF16)

    xp = x_prompt.reshape(B * S, D)
    xs = x_sample.reshape(DB * T, D)
    outs = {k: [] for k in ("v_a_s", "buf_b_p", "buf_b_s", "ckv_c_p", "kr_c_p", "ckv_c_s", "kr_c_s",
                            "k_d_p", "v_d_p", "logf_d_p", "k_d_s", "v_d_s", "logf_d_s")}
    for i in range(depth):
        kind, j = i % 4, i // 4
        g4 = g_norm[i]
        wu, wd = bf(w_up[i]), bf(w_down[i])
        if kind == 0:
            n_groups, chunk = w_s_a.shape[1], w_s_a.shape[2]
            d_a = w_out_a.shape[1]
            c_a = d_a // n_groups
            assert chunk % T == 0
            tril = jnp.tril(jnp.ones((chunk, chunk), bool))
            ws_p = bf(jnp.where(tril[None], w_s_a[j], 0.0))
            bias_p = jnp.repeat(b_s_a[j].T, c_a, axis=1)
            tril_t = jnp.tril(jnp.ones((T, T), bool))
            ws_t = jnp.where(tril_t[None], w_s_a[j][:, :T, :T], 0.0)
            eye = jnp.eye(chunk // T, dtype=F32)
            ws_s = bf(jnp.einsum("ab,gts->gatbs", eye, ws_t).reshape(n_groups, chunk, chunk))
            bias_s = jnp.tile(jnp.repeat(b_s_a[j][:, :T].T, c_a, axis=1), (chunk // T, 1))
            wa = (g4, bf(w_in_a[j]), g_v_a[j][None], b_v_a[j][None])
            xp, _ = _gmlp(xp, *wa, ws_p, bias_p, bf(w_out_a[j]), chunk)
            xs, v_new = _gmlp(xs, *wa, ws_s, bias_s, bf(w_out_a[j]), chunk)
            outs["v_a_s"].append(v_new.reshape(DB, T, d_a))
            xp = _ffn(xp, g4, wu, wd)
            xs = _ffn(xs, g4, wu, wd)
        elif kind == 1:
            wb = (g4, bf(w_in_b[j]), bf(w_grp_b[j]), scale_b[j][None])
            xp, buf_p = _pool_prompt(xp, *wb, n_seq=B)
            outs["buf_b_p"].append(buf_p[:, HIST_ROWS - POOL_HIST:, :])
            xs_tm = xs.reshape(DB, T, D).transpose(1, 0, 2).reshape(T * DB, D)
            hist_tm = state_b_buf[j].transpose(1, 0, 2).reshape(POOL_HIST * DB, D)
            xs_tm, z_tm = _pool_sample(xs_tm, hist_tm, *wb, db=DB, t_new=T, pos0=past)
            xs = xs_tm.reshape(T, DB, D).transpose(1, 0, 2).reshape(DB * T, D)
            z_s = z_tm.reshape(T, DB, D).transpose(1, 0, 2)
            outs["buf_b_s"].append(jnp.concatenate([state_b_buf[j], z_s], axis=1)[:, -POOL_HIST:])
            xp = _ffn(xp, g4, wu, wd)
            xs = _ffn(xs, g4, wu, wd)
        elif kind == 2:
            n_heads, kv_rank, dn = w_uk_c.shape[1:]
            dv = w_uv_c.shape[3]
            dr = cache_kr_c.shape[3]
            q_rank = g_q_c.shape[1]
            scale = float((dn + dr) ** -0.5) * LOG2E
            wuq3 = w_uq_c[j].reshape(q_rank, n_heads, dn + dr)
            wuq_perm = bf(jnp.concatenate([wuq3[:, :, :dn].reshape(q_rank, -1),
                                           wuq3[:, :, dn:dn + dr // 2].reshape(q_rank, -1),
                                           wuq3[:, :, dn + dr // 2:].reshape(q_rank, -1)], axis=1))
            wukt = bf(w_uk_c[j].transpose(0, 2, 1))
            wuvt = bf(w_uv_c[j].transpose(0, 2, 1))
            wuv_all = bf(w_uv_c[j].transpose(1, 0, 2).reshape(kv_rank, n_heads * dv))
            wc = (g4, bf(w_in_c[j]), g_q_c[j][None], g_kv_c[j][None], wuq_perm, wukt)
            cos_p, sin_p = _rope_tables(jnp.tile(pos_p, B), dr, n_heads)
            q_cat, kcat, ckvt, ckv, kr = _mla_proj(xp, *wc, cos_p, sin_p, scale)
            o_p = _mla_flash(q_cat, kcat, ckvt, wuvt, n_seq=B)
            outs["ckv_c_p"].append(ckv.reshape(B, S, kv_rank))
            outs["kr_c_p"].append(kr.reshape(B, S, dr))
            cos_s, sin_s = _rope_tables(jnp.tile(pos_s, DB), dr, n_heads)
            q_cat_s, kcat_s, _, ckv_s, kr_s = _mla_proj(xs, *wc, cos_s, sin_s, scale)
            dk = kv_rank + dr
            q_rows = q_cat_s.reshape(n_heads, DB, T, dk).transpose(1, 2, 0, 3).reshape(DB * T * n_heads, dk)
            knew = _pad_rows(kcat_s.reshape(DB, T, dk), BF16_ROWS)
            o_s = _mla_decode(q_rows, knew, wuv_all, cache_ckv_c, cache_kr_c.transpose(0, 1, 3, 2), j,
                              page_table, T, n_heads)
            outs["ckv_c_s"].append(ckv_s.reshape(DB, T, kv_rank))
            outs["kr_c_s"].append(kr_s.reshape(DB, T, dr))
            xp = _ffn(xp, g4, wu, wd, a=o_p, wo=bf(w_o_c[j]))
            xs = _ffn(xs, g4, wu, wd, a=o_s.reshape(DB * T, n_heads * dv), wo=bf(w_o_c[j]))
        else:
            n_heads, dh = cache_k_d.shape[3], cache_k_d.shape[4]
            hd = n_heads * dh
            scale = float(dh ** -0.5) * LOG2E
            wdd = (g4, bf(w_in_d[j]), b_f_d[j][None], scale)
            q, lf, _, kt, vt, kb, vtb, qa, ka = _fox_proj(xp, *wdd, n_seq=B)
            o_p = _fox_flash(q, qa, kb, ka, vtb, n_seq=B, dh=dh)
            outs["k_d_p"].append(kt.reshape(B, n_heads, dh, S).transpose(0, 3, 1, 2))
            outs["v_d_p"].append(vt.reshape(B, n_heads, dh, S).transpose(0, 3, 1, 2))
            outs["logf_d_p"].append(lf.reshape(B, S, n_heads))
            q_s, lf_s, cs, k_s, v_s, kb_s, vb_s = _fox_proj(xs, *wdd, seg=T)
            cs3 = cs.reshape(DB, T, n_heads)
            cs_t = jnp.pad(cs3.transpose(0, 2, 1), ((0, 0), (0, 0), (0, BF16_ROWS - T)))
            o_s = _fox_decode(q_s.astype(F32).reshape(DB, T, hd), cs3.reshape(DB, T * n_heads, 1), cs_t,
                              _pad_rows(kb_s.reshape(DB, T, hd), BF16_ROWS),
                              _pad_rows(vb_s.reshape(DB, T, hd), BF16_ROWS),
                              cache_k_d.transpose(0, 1, 3, 4, 2), cache_v_d.transpose(0, 1, 3, 4, 2),
                              cache_logf_d.transpose(0, 1, 3, 2), j, page_table)
            outs["k_d_s"].append(k_s.reshape(DB, T, n_heads, dh))
            outs["v_d_s"].append(v_s.reshape(DB, T, n_heads, dh))
            outs["logf_d_s"].append(lf_s.reshape(DB, T, n_heads))
            xp = _ffn(xp, g4, wu, wd, a=o_p, wo=bf(w_o_d[j]))
            xs = _ffn(xs, g4, wu, wd, a=o_s.reshape(DB * T, hd), wo=bf(w_o_d[j]))
    st = lambda name: jnp.stack(outs[name])
    return (xp.reshape(B, S, D), xs.reshape(DB, T, D), st("v_a_s"), st("buf_b_p"), st("buf_b_s"),
            st("ckv_c_p"), st("kr_c_p"), st("ckv_c_s"), st("kr_c_s"),
            st("k_d_p"), st("v_d_p"), st("logf_d_p"), st("k_d_s"), st("v_d_s"), st("logf_d_s"))
```

```python
import functools

import jax
import jax.numpy as jnp
import numpy as np
from jax import lax
from jax.experimental import pallas as pl
from jax.experimental.pallas import tpu as pltpu

F32 = jnp.float32
BF16 = jnp.bfloat16
EPS = 1e-6
LOG2E = 1.4426950408889634
ROPE_THETA = 10000.0
POOL_WINDOWS = (2, 4, 8, 16)
POOL_HIST = max(POOL_WINDOWS) - 1
HIST_ROWS = 16
BF16_ROWS = 16
V7X_VMEM_LIMIT = 56 * 1024 * 1024
FOX_PAGES_PER_STEP = 16
MLA_PAGES_PER_STEP = 32
FFN_CHUNK = 1024
MLA_CHUNK_COLS = 4096
FOX_CHUNK_COLS = 1024


def _cparams(*sem):
    return pltpu.CompilerParams(dimension_semantics=sem, vmem_limit_bytes=V7X_VMEM_LIMIT)


def _const_spec(shape):
    nd = len(shape)
    return pl.BlockSpec(shape, lambda *_: (0,) * nd, pipeline_mode=pl.Buffered(1))


def _rms(x, g):
    return x * lax.rsqrt(jnp.mean(x * x, axis=-1, keepdims=True) + EPS) * g


def _dot(a, b):
    return jnp.dot(a, b, preferred_element_type=F32)


def _dot_nt(a, b):
    return lax.dot_general(a, b, (((1,), (1,)), ((), ())), preferred_element_type=F32)


def _split3(x):
    hi = x.astype(BF16)
    r = x - hi.astype(F32)
    mid = r.astype(BF16)
    lo = (r - mid.astype(F32)).astype(BF16)
    return hi, mid, lo


def _cumsum_rows(x, seg=None):
    n = x.shape[0]
    r = lax.broadcasted_iota(jnp.int32, (n, n), 0)
    c = lax.broadcasted_iota(jnp.int32, (n, n), 1)
    keep = c <= r
    if seg is not None and seg < n:
        keep = keep & (c // seg == r // seg)
    tri = jnp.where(keep, 1.0, 0.0).astype(BF16)
    hi, mid, lo = _split3(x)
    return _dot(tri, hi) + _dot(tri, mid) + _dot(tri, lo)


def _online_softmax_cols(st, vt, m_prev, l_prev):
    m_new = jnp.maximum(m_prev, jnp.max(st, axis=0, keepdims=True))
    alpha = jnp.exp2(m_prev - m_new)
    p = jnp.exp2(st - m_new).astype(BF16)
    dv = vt.shape[0]
    pv = _dot(jnp.concatenate([vt, jnp.ones((BF16_ROWS, vt.shape[1]), BF16)], axis=0), p)
    l_new = alpha * l_prev + pv[dv:dv + 1, :]
    return m_new, l_new, alpha, pv[:dv, :]


def _online_softmax_step(s, v, m_ref, l_ref, acc_ref):
    m_prev = m_ref[...]
    m_new = jnp.maximum(m_prev, jnp.max(s, axis=1, keepdims=True))
    alpha = jnp.exp2(m_prev - m_new)
    p = jnp.exp2(s - m_new)
    l_ref[...] = alpha * l_ref[...] + jnp.sum(p, axis=1, keepdims=True)
    acc_ref[...] = alpha * acc_ref[...] + _dot(p.astype(BF16), v)
    m_ref[...] = m_new


def _init_softmax_state(m_ref, l_ref, acc_ref):
    m_ref[...] = jnp.full(m_ref.shape, -jnp.inf, F32)
    l_ref[...] = jnp.zeros(l_ref.shape, F32)
    acc_ref[...] = jnp.zeros(acc_ref.shape, F32)


def _ffn_tail(x1, g_ref, wu_ref, wd_ref):
    f = wu_ref.shape[1]
    fc = min(FFN_CHUNK, f)
    h = _rms(x1, g_ref[2:3, :]).astype(BF16)
    acc = None
    for c in range(f // fc):
        u = jnp.maximum(_dot(h, wu_ref[:, c * fc:(c + 1) * fc]), 0.0)
        d = _dot((u * u).astype(BF16), wd_ref[c * fc:(c + 1) * fc, :])
        acc = d if acc is None else acc + d
    return x1 + _rms(acc, g_ref[3:4, :])


def _ffn_kernel(*refs, fc, with_proj):
    if with_proj:
        a_ref, x_ref, wo_ref, g_ref, wu_ref, wd_ref, o_ref = refs
        x1 = x_ref[...] + _rms(_dot(a_ref[...], wo_ref[...]), g_ref[1:2, :])
    else:
        x_ref, g_ref, wu_ref, wd_ref, o_ref = refs
        x1 = x_ref[...]
    o_ref[...] = _ffn_tail(x1, g_ref, wu_ref, wd_ref)


def _ffn(x, g4, wu, wd, a=None, wo=None, tm=512):
    n, d = x.shape
    tm = min(tm, n)
    assert n % tm == 0
    f = wu.shape[1]
    fc = min(1024, f)
    assert f % fc == 0
    row = lambda i: (i, 0)
    specs, args = [], []
    if a is not None:
        specs += [pl.BlockSpec((tm, a.shape[1]), row), pl.BlockSpec((tm, d), row), _const_spec(wo.shape)]
        args += [a, x, wo]
    else:
        specs += [pl.BlockSpec((tm, d), row)]
        args += [x]
    specs += [_const_spec(g4.shape), _const_spec(wu.shape), _const_spec(wd.shape)]
    args += [g4, wu, wd]
    return pl.pallas_call(
        functools.partial(_ffn_kernel, fc=fc, with_proj=a is not None), name="ffn",
        grid=(n // tm,), in_specs=specs, out_specs=pl.BlockSpec((tm, d), row),
        out_shape=jax.ShapeDtypeStruct((n, d), F32), compiler_params=_cparams("parallel"))(*args)


def _gmlp_kernel(x_ref, g_ref, win_ref, gv_ref, bv_ref, ws_ref, bias_ref, wout_ref, wu_ref, wd_ref, o_ref, v_ref,
                 *, chunk):
    x = x_ref[...]
    d_a = wout_ref.shape[0]
    n_groups = ws_ref.shape[0]
    c_a = d_a // n_groups
    uv = jax.nn.gelu(_dot(_rms(x, g_ref[0:1, :]).astype(BF16), win_ref[...]))
    u = uv[:, :d_a]
    v = uv[:, d_a:]
    vc = v - jnp.mean(v, axis=-1, keepdims=True)
    v = vc * lax.rsqrt(jnp.mean(vc * vc, axis=-1, keepdims=True) + EPS) * gv_ref[...] + bv_ref[...]
    v_ref[...] = v
    vb = v.astype(BF16)
    rows = []
    for c in range(x.shape[0] // chunk):
        cols = [_dot(ws_ref[g], vb[c * chunk:(c + 1) * chunk, g * c_a:(g + 1) * c_a]) for g in range(n_groups)]
        rows.append(jnp.concatenate(cols, axis=1))
    mixed = jnp.concatenate(rows, axis=0) + bias_ref[...]
    m = _dot((u * mixed).astype(BF16), wout_ref[...])
    o_ref[...] = _ffn_tail(x + _rms(m, g_ref[1:2, :]), g_ref, wu_ref, wd_ref)


def _gmlp(x, g4, w_in, g_v, b_v, ws_eff, bias_rows, w_out, wu, wd, chunk, tm=256):
    n, d = x.shape
    tm = min(tm, n)
    assert n % tm == 0 and tm % chunk == 0
    d_a = w_out.shape[0]
    bias = jnp.tile(bias_rows, (tm // chunk, 1))
    row = lambda i: (i, 0)
    return pl.pallas_call(
        functools.partial(_gmlp_kernel, chunk=chunk), name="gmlp",
        grid=(n // tm,),
        in_specs=[pl.BlockSpec((tm, d), row), _const_spec(g4.shape), _const_spec(w_in.shape),
                  _const_spec(g_v.shape), _const_spec(b_v.shape), _const_spec(ws_eff.shape),
                  _const_spec(bias.shape), _const_spec(w_out.shape), _const_spec(wu.shape), _const_spec(wd.shape)],
        out_specs=[pl.BlockSpec((tm, d), row), pl.BlockSpec((tm, d_a), row)],
        out_shape=[jax.ShapeDtypeStruct((n, d), F32), jax.ShapeDtypeStruct((n, d_a), F32)],
        compiler_params=_cparams("parallel"))(x, g4, w_in, g_v, b_v, ws_eff, bias, w_out, wu, wd)


def _pool_finish(x, dmat, g_ref, wg_ref, sc_ref):
    n_g = wg_ref.shape[0]
    cb = x.shape[1] // n_g
    ys = [_dot(dmat[:, gi * cb:(gi + 1) * cb], wg_ref[gi]) for gi in range(n_g)]
    y = jnp.concatenate(ys, axis=1) * sc_ref[...]
    return x + _rms(y, g_ref[1:2, :])


def _pool_prompt_kernel(x_ref, g_ref, win_ref, wg_ref, sc_ref, wu_ref, wd_ref, o_ref, buf_ref, zbuf,
                        *, tiles_per_seq):
    tm, d = x_ref.shape
    li = pl.program_id(0) % tiles_per_seq

    @pl.when(li == 0)
    def _():
        zbuf[0:HIST_ROWS, :] = jnp.zeros((HIST_ROWS, d), F32)

    x = x_ref[...]
    z = _dot(_rms(x, g_ref[0:1, :]).astype(BF16), win_ref[...])
    zbuf[HIST_ROWS:HIST_ROWS + tm, :] = z
    pos = li * tm + lax.broadcasted_iota(jnp.int32, (tm, 1), 0)
    cb = d // len(POOL_WINDOWS)
    cols = []
    for gi, w in enumerate(POOL_WINDOWS):
        sl = slice(gi * cb, (gi + 1) * cb)
        zg = z[:, sl]
        acc = zg
        for k in range(1, w):
            acc = acc + zbuf[HIST_ROWS - k:HIST_ROWS - k + tm, sl]
        cnt = jnp.minimum(w, pos + 1).astype(F32)
        cols.append(acc / cnt - zg)
    dmat = jnp.concatenate(cols, axis=1).astype(BF16)
    o_ref[...] = _ffn_tail(_pool_finish(x, dmat, g_ref, wg_ref, sc_ref), g_ref, wu_ref, wd_ref)
    tail = z[tm - HIST_ROWS:, :]
    zbuf[0:HIST_ROWS, :] = tail
    buf_ref[0] = tail


def _pool_prompt(x, g4, w_in, w_grp, scale, wu, wd, n_seq, tm=256):
    n, d = x.shape
    s = n // n_seq
    tm = min(tm, s)
    assert s % tm == 0 and tm >= HIST_ROWS
    tps = s // tm
    row = lambda i: (i, 0)
    return pl.pallas_call(
        functools.partial(_pool_prompt_kernel, tiles_per_seq=tps), name="pool_prompt",
        grid=(n // tm,),
        in_specs=[pl.BlockSpec((tm, d), row), _const_spec(g4.shape), _const_spec(w_in.shape),
                  _const_spec(w_grp.shape), _const_spec(scale.shape), _const_spec(wu.shape), _const_spec(wd.shape)],
        out_specs=[pl.BlockSpec((tm, d), row), pl.BlockSpec((1, HIST_ROWS, d), lambda i: (i // tps, 0, 0))],
        out_shape=[jax.ShapeDtypeStruct((n, d), F32), jax.ShapeDtypeStruct((n_seq, HIST_ROWS, d), F32)],
        scratch_shapes=[pltpu.VMEM((HIST_ROWS + tm, d), F32)],
        compiler_params=_cparams("arbitrary"))(x, g4, w_in, w_grp, scale, wu, wd)


def _pool_sample_kernel(x_ref, hist_ref, g_ref, win_ref, wg_ref, sc_ref, o_ref, z_ref, *, db, t_new, pos0):
    x = x_ref[...]
    d = x.shape[1]
    z = _dot(_rms(x, g_ref[0:1, :]).astype(BF16), win_ref[...])
    z_ref[...] = z
    cb = d // len(POOL_WINDOWS)

    def rows_at(p, sl):
        if p >= POOL_HIST:
            return z[(p - POOL_HIST) * db:(p - POOL_HIST + 1) * db, sl]
        return hist_ref[p * db:(p + 1) * db, sl]

    drows = []
    for t in range(t_new):
        cols = []
        for gi, w in enumerate(POOL_WINDOWS):
            sl = slice(gi * cb, (gi + 1) * cb)
            acc = rows_at(POOL_HIST + t, sl)
            for k in range(1, w):
                acc = acc + rows_at(POOL_HIST + t - k, sl)
            cnt = float(min(w, pos0 + t + 1))
            cols.append(acc / cnt - z[t * db:(t + 1) * db, sl])
        drows.append(jnp.concatenate(cols, axis=1))
    dmat = jnp.concatenate(drows, axis=0).astype(BF16)
    o_ref[...] = _pool_finish(x, dmat, g_ref, wg_ref, sc_ref)


def _pool_sample(x_tm, hist_tm, g4, w_in, w_grp, scale, db, t_new, pos0):
    n, d = x_tm.shape
    args = (x_tm, hist_tm, g4, w_in, w_grp, scale)
    return pl.pallas_call(
        functools.partial(_pool_sample_kernel, db=db, t_new=t_new, pos0=pos0), name="pool_sample",
        grid=(1,),
        in_specs=[_const_spec(a.shape) for a in args],
        out_specs=[pl.BlockSpec((n, d), lambda i: (0, 0)), pl.BlockSpec((n, d), lambda i: (0, 0))],
        out_shape=[jax.ShapeDtypeStruct((n, d), F32), jax.ShapeDtypeStruct((n, d), F32)],
        compiler_params=_cparams("arbitrary"))(*args)


def _mla_proj_kernel(x_ref, g_ref, win_ref, gq_ref, gkv_ref, wuq_ref, wukt_ref, cos_ref, sin_ref,
                     q_ref, kcat_ref, ckvt_ref, ckv_ref, kr_ref, *, q_rank, kv_rank, scale):
    n_heads, dn, _ = wukt_ref.shape
    x = x_ref[...]
    c = _dot(_rms(x, g_ref[0:1, :]).astype(BF16), win_ref[...])
    c_q = c[:, :q_rank]
    c_kv = c[:, q_rank:q_rank + kv_rank]
    k_r = c[:, q_rank + kv_rank:]
    q = _dot(_rms(c_q, gq_ref[...]).astype(BF16), wuq_ref[...])
    hd = n_heads * dn
    half = (q.shape[1] - hd) // 2
    hr = half // n_heads
    x1 = q[:, hd:hd + half]
    x2 = q[:, hd + half:]
    cosq = cos_ref[...]
    sinq = sin_ref[...]
    r1 = ((x1 * cosq - x2 * sinq) * scale).astype(BF16)
    r2 = ((x1 * sinq + x2 * cosq) * scale).astype(BF16)
    ckv = _rms(c_kv, gkv_ref[...])
    ckv_ref[...] = ckv
    ckvt_ref[...] = ckv.T.astype(BF16)
    k1 = k_r[:, :hr]
    k2 = k_r[:, hr:]
    c1 = cosq[:, :hr]
    s1 = sinq[:, :hr]
    kr = jnp.concatenate([k1 * c1 - k2 * s1, k1 * s1 + k2 * c1], axis=1)
    kr_ref[...] = kr
    kcat_ref[:, :kv_rank] = ckv.astype(BF16)
    kcat_ref[:, kv_rank:] = kr.astype(BF16)
    for h in range(n_heads):
        ql = _dot(q[:, h * dn:(h + 1) * dn].astype(BF16), wukt_ref[h]) * scale
        q_ref[h, :, :kv_rank] = ql.astype(BF16)
        q_ref[h, :, kv_rank:kv_rank + hr] = r1[:, h * hr:(h + 1) * hr]
        q_ref[h, :, kv_rank + hr:] = r2[:, h * hr:(h + 1) * hr]


def _mla_proj(x, g4, w_in, g_q, g_kv, wuq_perm, wukt, cosq, sinq, scale, tm=256):
    n, d = x.shape
    tm = min(tm, n)
    assert n % tm == 0
    n_heads, _, kv_rank = wukt.shape
    q_rank = g_q.shape[1]
    dr = w_in.shape[1] - q_rank - kv_rank
    dk = kv_rank + dr
    row = lambda i: (i, 0)
    return pl.pallas_call(
        functools.partial(_mla_proj_kernel, q_rank=q_rank, kv_rank=kv_rank, scale=scale), name="mla_proj",
        grid=(n // tm,),
        in_specs=[pl.BlockSpec((tm, d), row), _const_spec(g4.shape), _const_spec(w_in.shape),
                  _const_spec(g_q.shape), _const_spec(g_kv.shape), _const_spec(wuq_perm.shape),
                  _const_spec(wukt.shape), pl.BlockSpec((tm, cosq.shape[1]), row),
                  pl.BlockSpec((tm, sinq.shape[1]), row)],
        out_specs=[pl.BlockSpec((n_heads, tm, dk), lambda i: (0, i, 0)), pl.BlockSpec((tm, dk), row),
                   pl.BlockSpec((kv_rank, tm), lambda i: (0, i)),
                   pl.BlockSpec((tm, kv_rank), row), pl.BlockSpec((tm, dr), row)],
        out_shape=[jax.ShapeDtypeStruct((n_heads, n, dk), BF16), jax.ShapeDtypeStruct((n, dk), BF16),
                   jax.ShapeDtypeStruct((kv_rank, n), BF16),
                   jax.ShapeDtypeStruct((n, kv_rank), F32), jax.ShapeDtypeStruct((n, dr), F32)],
        compiler_params=_cparams("parallel"))(x, g4, w_in, g_q, g_kv, wuq_perm, wukt, cosq, sinq)


def _causal_tile_pairs(s, tq, tk):
    pairs = [(i, j) for i in range(s // tq) for j in range((i * tq + tq - 1) // tk + 1)]
    return (jnp.asarray([p[0] for p in pairs], jnp.int32), jnp.asarray([p[1] for p in pairs], jnp.int32))


def _mla_flash_kernel(qi_ref, kj_ref, q_ref, k_ref, vt_ref, wuvt_ref, o_ref, m_ref, l_ref, acc_ref, *, tq, tk):
    p = pl.program_id(1)
    i = qi_ref[p]
    j = kj_ref[p]
    n_heads = q_ref.shape[0]
    m_cols = n_heads * tq

    @pl.when(j == 0)
    def _():
        _init_softmax_state(m_ref, l_ref, acc_ref)

    def step(masked):
        k = k_ref[...]
        vt = vt_ref[...]
        hpc = max(1, min(n_heads, MLA_CHUNK_COLS // tq))
        cw = hpc * tq
        if masked:
            q_pos = i * tq + (lax.broadcasted_iota(jnp.int32, (1, cw), 1) & (tq - 1))
            causal = j * tk + lax.broadcasted_iota(jnp.int32, (tk, 1), 0) <= q_pos
        for c in range(n_heads // hpc):
            cols = slice(c * cw, (c + 1) * cw)
            st = _dot_nt(k, q_ref[c * hpc:(c + 1) * hpc].reshape(cw, q_ref.shape[2]))
            if masked:
                st = jnp.where(causal, st, -jnp.inf)
            m_new, l_new, alpha, pv = _online_softmax_cols(st, vt, m_ref[:, cols], l_ref[:, cols])
            m_ref[:, cols] = m_new
            l_ref[:, cols] = l_new
            acc_ref[:, cols] = alpha * acc_ref[:, cols] + pv

    crosses_diagonal = j * tk + tk - 1 > i * tq

    @pl.when(crosses_diagonal)
    def _():
        step(True)

    @pl.when(jnp.logical_not(crosses_diagonal))
    def _():
        step(False)

    @pl.when((j + 1) * tk > i * tq + tq - 1)
    def _():
        ot = (acc_ref[...] / l_ref[...]).astype(BF16)
        outs = [_dot(wuvt_ref[h], ot[:, h * tq:(h + 1) * tq]) for h in range(n_heads)]
        o_ref[...] = jnp.concatenate(outs, axis=0).T.astype(o_ref.dtype)


def _mla_flash(q_cat, kcat, ckvt, wuvt, n_seq, tq=256, tk=512):
    n_heads, n, dk = q_cat.shape
    s = n // n_seq
    tq = min(tq, s)
    tk = min(tk, s)
    assert s % tq == 0 and s % tk == 0 and tq & (tq - 1) == 0
    nq, nk = s // tq, s // tk
    dv, kv_rank = wuvt.shape[1], wuvt.shape[2]
    qi, kj = _causal_tile_pairs(s, tq, tk)
    grid_spec = pltpu.PrefetchScalarGridSpec(
        num_scalar_prefetch=2, grid=(n_seq, qi.shape[0]),
        in_specs=[pl.BlockSpec((n_heads, tq, dk), lambda b, p, qi, kj: (0, b * nq + qi[p], 0)),
                  pl.BlockSpec((tk, dk), lambda b, p, qi, kj: (b * nk + kj[p], 0)),
                  pl.BlockSpec((kv_rank, tk), lambda b, p, qi, kj: (0, b * nk + kj[p])),
                  _const_spec(wuvt.shape)],
        out_specs=pl.BlockSpec((tq, n_heads * dv), lambda b, p, qi, kj: (b * nq + qi[p], 0)),
        scratch_shapes=[pltpu.VMEM((1, n_heads * tq), F32), pltpu.VMEM((1, n_heads * tq), F32),
                        pltpu.VMEM((kv_rank, n_heads * tq), F32)])
    return pl.pallas_call(
        functools.partial(_mla_flash_kernel, tq=tq, tk=tk), name="mla_flash", grid_spec=grid_spec,
        out_shape=jax.ShapeDtypeStruct((n, n_heads * dv), BF16),
        compiler_params=_cparams("parallel", "arbitrary"))(qi, kj, q_cat, kcat, ckvt, wuvt)


def _head_diag(full, n_heads, t_new):
    width = full.shape[1]
    dv = width // n_heads
    own = (lax.broadcasted_iota(jnp.int32, (n_heads, width), 1) // dv
           == lax.broadcasted_iota(jnp.int32, (n_heads, width), 0))
    full3 = full.reshape(t_new, n_heads, width)
    return jnp.sum(jnp.where(own[None], full3, 0.0), axis=1)


def _mla_decode_kernel(pt_ref, q_ref, knew_ref, wuv_ref, *rest, n_b, n_pp, page, kv_rank, t_new, n_heads):
    n_pg = n_b * n_pp
    ckv_refs = rest[:n_pg]
    krt_refs = rest[n_pg:2 * n_pg]
    o_ref = rest[2 * n_pg]
    scratch = rest[2 * n_pg + 1:]
    state = [scratch[5 * u:5 * u + 5] for u in range(n_b)]
    j = pl.program_id(1)
    rows = t_new * n_heads

    @pl.when(j == 0)
    def _():
        for m_ref, l_ref, acc_ref, _, _ in state:
            _init_softmax_state(m_ref, l_ref, acc_ref)

    for u, (m_ref, l_ref, acc_ref, cbuf, rbuf) in enumerate(state):
        for r in range(n_pp):
            cbuf[r * page:(r + 1) * page, :] = ckv_refs[u * n_pp + r][0, 0].astype(BF16)
            rbuf[:, r * page:(r + 1) * page] = krt_refs[u * n_pp + r][0, 0].astype(BF16)
        q = q_ref[u * rows:(u + 1) * rows, :]
        cb = cbuf[...]
        s = _dot_nt(q[:, :kv_rank], cb) + _dot(q[:, kv_rank:], rbuf[...])
        _online_softmax_step(s, cb, m_ref, l_ref, acc_ref)

    @pl.when(j == pl.num_programs(1) - 1)
    def _():
        for u, (m_ref, l_ref, acc_ref, _, _) in enumerate(state):
            q = q_ref[u * rows:(u + 1) * rows, :]
            kn = knew_ref[u]
            s = _dot_nt(q, kn)
            q_t = lax.broadcasted_iota(jnp.int32, (s.shape[0], 1), 0) // n_heads
            k_t = lax.broadcasted_iota(jnp.int32, (1, s.shape[1]), 1)
            s = jnp.where(k_t <= q_t, s, -jnp.inf)
            _online_softmax_step(s, kn[:, :kv_rank], m_ref, l_ref, acc_ref)
            ol = (acc_ref[...] / l_ref[...]).astype(BF16)
            o_ref[u] = _head_diag(_dot(ol, wuv_ref[...]), n_heads, t_new).astype(o_ref.dtype)


def _mla_decode(q_rows, knew, wuv_all, cache_ckv, cache_krt, layer, page_table, t_new, n_heads, n_b=2):
    db, n_pages = page_table.shape
    page, kv_rank = cache_ckv.shape[2], cache_ckv.shape[3]
    dr = cache_krt.shape[2]
    dk = kv_rank + dr
    n_pp = min(MLA_PAGES_PER_STEP, n_pages)
    assert n_pages % n_pp == 0 and db % n_b == 0
    rows = t_new * n_heads
    width = wuv_all.shape[1]

    def page_spec(u, r, shape):
        return pl.BlockSpec((1, 1) + shape, lambda b, j, pt: (layer, pt[b * n_b + u, j * n_pp + r], 0, 0))

    in_specs = ([pl.BlockSpec((n_b * rows, dk), lambda b, j, pt: (b, 0)),
                 pl.BlockSpec((n_b, BF16_ROWS, dk), lambda b, j, pt: (b, 0, 0)),
                 _const_spec(wuv_all.shape)]
                + [page_spec(u, r, (page, kv_rank)) for u in range(n_b) for r in range(n_pp)]
                + [page_spec(u, r, (dr, page)) for u in range(n_b) for r in range(n_pp)])
    per_sample = [pltpu.VMEM((rows, 1), F32), pltpu.VMEM((rows, 1), F32), pltpu.VMEM((rows, kv_rank), F32),
                  pltpu.VMEM((n_pp * page, kv_rank), BF16), pltpu.VMEM((dr, n_pp * page), BF16)]
    grid_spec = pltpu.PrefetchScalarGridSpec(
        num_scalar_prefetch=1, grid=(db // n_b, n_pages // n_pp), in_specs=in_specs,
        out_specs=pl.BlockSpec((n_b, t_new, width), lambda b, j, pt: (b, 0, 0)),
        scratch_shapes=per_sample * n_b)
    return pl.pallas_call(
        functools.partial(_mla_decode_kernel, n_b=n_b, n_pp=n_pp, page=page, kv_rank=kv_rank, t_new=t_new,
                          n_heads=n_heads),
        name="mla_decode",
        grid_spec=grid_spec, out_shape=jax.ShapeDtypeStruct((db, t_new, width), BF16),
        compiler_params=_cparams("parallel", "arbitrary"))(
            page_table, q_rows, knew, wuv_all, *([cache_ckv] * (n_b * n_pp)), *([cache_krt] * (n_b * n_pp)))


def _fox_proj_kernel(x_ref, g_ref, win_ref, bf_ref, place_ref, ones_ref, q_ref, lf_ref, c_ref, *rest,
                     hd, scale, seg, tiles_per_seq):
    x = x_ref[...]
    y = _dot(_rms(x, g_ref[0:1, :]).astype(BF16), win_ref[...])
    q_ref[...] = (y[:, :hd] * scale).astype(BF16)
    k = y[:, hd:2 * hd]
    v = y[:, 2 * hd:3 * hd]
    lf = jax.nn.log_sigmoid(y[:, 3 * hd:] + bf_ref[...])
    lf_ref[...] = lf
    c = _cumsum_rows(lf, seg)
    if tiles_per_seq is None:
        k_ref, v_ref, kb_ref, vb_ref = rest
        k_ref[...] = k
        v_ref[...] = v
        kb_ref[...] = k.astype(BF16)
        vb_ref[...] = v.astype(BF16)
    else:
        kt_ref, vt_ref, kb_ref, vtb_ref, qa_ref, ka_ref, carry = rest

        @pl.when(pl.program_id(0) % tiles_per_seq == 0)
        def _():
            carry[...] = jnp.zeros(carry.shape, F32)

        c = c + carry[...]
        carry[...] = c[c.shape[0] - 1:, :]
        vt = v.T
        kt_ref[0] = k.T
        vt_ref[0] = vt
        kb_ref[...] = k.astype(BF16)
        vtb_ref[0] = vt.astype(BF16)
        n_heads = c.shape[1]
        parts = _split3(c * LOG2E)
        for side, out_ref in ((0, qa_ref), (1, ka_ref)):
            aug = ones_ref[side:side + 1, :]
            for t, part in enumerate(parts):
                aug = aug + _dot(part, place_ref[side, t * n_heads:(t + 1) * n_heads, :])
            out_ref[...] = aug.astype(BF16)
    c_ref[...] = c


def _fox_proj(x, g4, w_in, b_f, scale, n_seq=None, seg=None, tm=256):
    n, d = x.shape
    n_heads = b_f.shape[1]
    hd = (w_in.shape[1] - n_heads) // 3
    s = n // n_seq if n_seq else n
    tm = min(tm, s)
    assert s % tm == 0 and (n_seq or tm % seg == 0)
    tps = s // tm
    row = lambda i: (i, 0)
    wide = pl.BlockSpec((tm, hd), row)
    narrow = pl.BlockSpec((tm, n_heads), row)
    out_specs = [wide, narrow, narrow]
    out_shape = [jax.ShapeDtypeStruct((n, hd), BF16), jax.ShapeDtypeStruct((n, n_heads), F32),
                 jax.ShapeDtypeStruct((n, n_heads), F32)]
    scratch = []
    if n_seq:
        tspec = pl.BlockSpec((1, hd, tm), lambda i: (i // tps, 0, i % tps))
        out_specs += [tspec, tspec, wide, tspec, wide, wide]
        out_shape += [jax.ShapeDtypeStruct((n_seq, hd, s), F32), jax.ShapeDtypeStruct((n_seq, hd, s), F32),
                      jax.ShapeDtypeStruct((n, hd), BF16), jax.ShapeDtypeStruct((n_seq, hd, s), BF16),
                      jax.ShapeDtypeStruct((n, hd), BF16), jax.ShapeDtypeStruct((n, hd), BF16)]
        scratch.append(pltpu.VMEM((1, n_heads), F32))
    else:
        out_specs += [wide, wide, wide, wide]
        out_shape += [jax.ShapeDtypeStruct((n, hd), F32), jax.ShapeDtypeStruct((n, hd), F32),
                      jax.ShapeDtypeStruct((n, hd), BF16), jax.ShapeDtypeStruct((n, hd), BF16)]
    place, ones = _fox_aug_tables(n_heads, hd // n_heads)
    return pl.pallas_call(
        functools.partial(_fox_proj_kernel, hd=hd, scale=scale, seg=seg, tiles_per_seq=tps if n_seq else None),
        name="fox_proj",
        grid=(n // tm,),
        in_specs=[pl.BlockSpec((tm, d), row), _const_spec(g4.shape), _const_spec(w_in.shape),
                  _const_spec(b_f.shape), _const_spec(place.shape), _const_spec(ones.shape)],
        out_specs=out_specs, out_shape=out_shape, scratch_shapes=scratch,
        compiler_params=_cparams("arbitrary"))(x, g4, w_in, b_f, place, ones)


FOX_LANES = 128
FOX_AUG_STRIDE = 8


def _fox_aug_tables(n_heads, dh):
    n_sub = FOX_LANES // dh
    place = np.zeros((2, 3 * n_heads, n_heads * dh), np.float32)
    ones = np.zeros((2, n_heads * dh), np.float32)
    for head in range(n_heads):
        base = (head // n_sub) * FOX_LANES + (head % n_sub) * FOX_AUG_STRIDE
        for t in range(3):
            place[0, t * n_heads + head, base + 3 + t] = 1.0
            place[1, t * n_heads + head, base + t] = -1.0
            ones[0, base + t] = 1.0
            ones[1, base + 3 + t] = 1.0
    return jnp.asarray(place, BF16), jnp.asarray(ones, F32)


def _fox_flash_kernel(qi_ref, kj_ref, q_ref, qa_ref, k_ref, ka_ref, vt_ref, o_ref, m_ref, l_ref, acc_ref,
                      *, tq, tk, dh):
    p = pl.program_id(2)
    i = qi_ref[p]
    j = kj_ref[p]
    width = q_ref.shape[1]
    n_sub = width // dh
    lane = lax.broadcasted_iota(jnp.int32, (1, 2 * width), 1)

    @pl.when(j == 0)
    def _():
        _init_softmax_state(m_ref, l_ref, acc_ref)

    def step(masked):
        k = jnp.concatenate([k_ref[...], ka_ref[...]], axis=1)
        cw = min(tq, FOX_CHUNK_COLS)
        for hh in range(n_sub):
            own = ((lane // dh == hh)
                   | ((lane >= width + hh * FOX_AUG_STRIDE) & (lane < width + (hh + 1) * FOX_AUG_STRIDE)))
            km = jnp.where(own, k, jnp.zeros_like(k))
            rows = slice(hh * dh, (hh + 1) * dh)
            vt = vt_ref[0, rows, :]
            for c in range(tq // cw):
                cols = slice(c * cw, (c + 1) * cw)
                st = _dot_nt(km, jnp.concatenate([q_ref[cols, :], qa_ref[cols, :]], axis=1))
                if masked:
                    causal = (j * tk + lax.broadcasted_iota(jnp.int32, (tk, 1), 0)
                              <= i * tq + c * cw + lax.broadcasted_iota(jnp.int32, (1, cw), 1))
                    st = jnp.where(causal, st, -jnp.inf)
                m_new, l_new, alpha, pv = _online_softmax_cols(st, vt, m_ref[hh, :, cols], l_ref[hh, :, cols])
                m_ref[hh, :, cols] = m_new
                l_ref[hh, :, cols] = l_new
                acc_ref[rows, cols] = alpha * acc_ref[rows, cols] + pv

    crosses_diagonal = j * tk + tk - 1 > i * tq

    @pl.when(crosses_diagonal)
    def _():
        step(True)

    @pl.when(jnp.logical_not(crosses_diagonal))
    def _():
        step(False)

    @pl.when((j + 1) * tk > i * tq + tq - 1)
    def _():
        outs = [acc_ref[hh * dh:(hh + 1) * dh, :] / l_ref[hh] for hh in range(n_sub)]
        o_ref[...] = jnp.concatenate(outs, axis=0).T.astype(o_ref.dtype)


def _fox_flash(q, qa, kb, ka, vtb, n_seq, dh, tq=1024, tk=1024):
    n, hd = q.shape
    s = n // n_seq
    tq = min(tq, s)
    tk = min(tk, s)
    lanes = FOX_LANES
    assert s % tq == 0 and s % tk == 0 and hd % lanes == 0 and lanes % dh == 0
    nq, nk = s // tq, s // tk
    n_sub = lanes // dh
    qi, kj = _causal_tile_pairs(s, tq, tk)
    q_spec = pl.BlockSpec((tq, lanes), lambda b, g, p, qi, kj: (b * nq + qi[p], g))
    k_spec = pl.BlockSpec((tk, lanes), lambda b, g, p, qi, kj: (b * nk + kj[p], g))
    grid_spec = pltpu.PrefetchScalarGridSpec(
        num_scalar_prefetch=2, grid=(n_seq, hd // lanes, qi.shape[0]),
        in_specs=[q_spec, q_spec, k_spec, k_spec,
                  pl.BlockSpec((1, lanes, tk), lambda b, g, p, qi, kj: (b, g, kj[p]))],
        out_specs=q_spec,
        scratch_shapes=[pltpu.VMEM((n_sub, 1, tq), F32), pltpu.VMEM((n_sub, 1, tq), F32),
                        pltpu.VMEM((lanes, tq), F32)])
    return pl.pallas_call(
        functools.partial(_fox_flash_kernel, tq=tq, tk=tk, dh=dh), name="fox_flash", grid_spec=grid_spec,
        out_shape=jax.ShapeDtypeStruct((n, hd), BF16),
        compiler_params=_cparams("parallel", "parallel", "arbitrary"))(qi, kj, q, qa, kb, ka, vtb)


def _fox_decode_kernel(pt_ref, q_ref, cs_ref, cst_ref, knew_ref, vnew_ref, suf_ref, *rest,
                       n_pp, page, n_heads, t_new):
    kt_refs = rest[:n_pp]
    vt_refs = rest[n_pp:2 * n_pp]
    lft_refs = rest[2 * n_pp:3 * n_pp]
    o_ref = rest[3 * n_pp]
    m_ref, l_ref, acc_ref, carry, qb_ref, kbuf, vbuf = rest[3 * n_pp + 1:]
    j = pl.program_id(1)
    hd = q_ref.shape[2]
    rows = t_new * n_heads
    dh = hd // n_heads

    @pl.when(j == 0)
    def _():
        _init_softmax_state(m_ref, l_ref, acc_ref)
        carry[...] = jnp.zeros(carry.shape, F32)
        own = (lax.broadcasted_iota(jnp.int32, (n_heads, hd), 1) // dh
               == lax.broadcasted_iota(jnp.int32, (n_heads, hd), 0))
        q = q_ref[0]
        qrep = jnp.where(own[None], jnp.broadcast_to(q[:, None, :], (t_new, n_heads, hd)), 0.0)
        qb_ref[...] = qrep.reshape(rows, hd).astype(BF16)

    qb = qb_ref[...]
    cs = cs_ref[0] * LOG2E

    parts = []
    for r in range(n_pp):
        parts += list(_split3(lft_refs[r][0, 0]))
    sums = _dot(jnp.concatenate(parts, axis=0), suf_ref[...])
    cur = carry[...]
    bias = []
    for r in range(n_pp):
        blk = sums[3 * n_heads * r:3 * n_heads * (r + 1), :]
        blk = blk[:n_heads] + blk[n_heads:2 * n_heads] + blk[2 * n_heads:]
        bias.append(blk[:, :page] + cur)
        cur = cur + blk[:, page:page + 1]
        kbuf[:, r * page:(r + 1) * page] = kt_refs[r][0, 0].reshape(hd, page).astype(BF16)
        vbuf[:, r * page:(r + 1) * page] = vt_refs[r][0, 0].reshape(hd, page).astype(BF16)
    carry[...] = cur
    after = jnp.concatenate(bias, axis=1) * LOG2E
    s = _dot(qb, kbuf[...]) + (cs + jnp.concatenate([after] * t_new, axis=0))
    m_prev = m_ref[...]
    m_new = jnp.maximum(m_prev, jnp.max(s, axis=1, keepdims=True))
    alpha = jnp.exp2(m_prev - m_new)
    p = jnp.exp2(s - m_new)
    l_ref[...] = alpha * l_ref[...] + jnp.sum(p, axis=1, keepdims=True)
    m_ref[...] = m_new
    acc_ref[...] = alpha * acc_ref[...] + _dot_nt(p.astype(BF16), vbuf[...])

    @pl.when(j == pl.num_programs(1) - 1)
    def _():
        kn = knew_ref[0]
        s2 = _dot_nt(qb, kn) + (cs - jnp.concatenate([cst_ref[0] * LOG2E] * t_new, axis=0))
        q_t = lax.broadcasted_iota(jnp.int32, (rows, 1), 0) // n_heads
        k_t = lax.broadcasted_iota(jnp.int32, (1, s2.shape[1]), 1)
        s2 = jnp.where(k_t <= q_t, s2, -jnp.inf)
        _online_softmax_step(s2, vnew_ref[0], m_ref, l_ref, acc_ref)
        o_ref[0] = _head_diag(acc_ref[...] / l_ref[...], n_heads, t_new).astype(o_ref.dtype)


def _fox_decode(q_s, cs_rows, cs_t, knew, vnew, cache_kt, cache_vt, cache_lft, layer, page_table):
    db, n_pages = page_table.shape
    n_heads, dh, page = cache_kt.shape[2:]
    hd = n_heads * dh
    t_new = q_s.shape[1]
    n_pp = min(FOX_PAGES_PER_STEP, n_pages)
    assert n_pages % n_pp == 0
    rows = t_new * n_heads
    per_b = lambda b, j, pt: (b, 0, 0)
    suf = jnp.concatenate([jnp.tril(jnp.ones((page, page), F32), -1),
                           jnp.ones((page, 1), F32), jnp.zeros((page, page - 1), F32)], axis=1).astype(BF16)

    def page_spec(r, shape):
        nd = len(shape)
        return pl.BlockSpec((1, 1) + shape,
                            lambda b, j, pt: (layer, pt[b, n_pages - 1 - (j * n_pp + r)]) + (0,) * nd)

    in_specs = ([pl.BlockSpec((1, t_new, hd), per_b), pl.BlockSpec((1, rows, 1), per_b),
                 pl.BlockSpec((1, n_heads, BF16_ROWS), per_b),
                 pl.BlockSpec((1, BF16_ROWS, hd), per_b), pl.BlockSpec((1, BF16_ROWS, hd), per_b),
                 _const_spec(suf.shape)]
                + [page_spec(r, (n_heads, dh, page)) for r in range(n_pp)] * 2
                + [page_spec(r, (n_heads, page)) for r in range(n_pp)])
    grid_spec = pltpu.PrefetchScalarGridSpec(
        num_scalar_prefetch=1, grid=(db, n_pages // n_pp), in_specs=in_specs,
        out_specs=pl.BlockSpec((1, t_new, hd), per_b),
        scratch_shapes=[pltpu.VMEM((rows, 1), F32), pltpu.VMEM((rows, 1), F32), pltpu.VMEM((rows, hd), F32),
                        pltpu.VMEM((n_heads, 1), F32), pltpu.VMEM((rows, hd), BF16),
                        pltpu.VMEM((hd, n_pp * page), BF16), pltpu.VMEM((hd, n_pp * page), BF16)])
    return pl.pallas_call(
        functools.partial(_fox_decode_kernel, n_pp=n_pp, page=page, n_heads=n_heads, t_new=t_new),
        name="fox_decode",
        grid_spec=grid_spec, out_shape=jax.ShapeDtypeStruct((db, t_new, hd), BF16),
        compiler_params=_cparams("parallel", "arbitrary"))(
            page_table, q_s, cs_rows, cs_t, knew, vnew, suf,
            *([cache_kt] * n_pp), *([cache_vt] * n_pp), *([cache_lft] * n_pp))


def _pad_rows(a, rows):
    return jnp.pad(a, ((0, 0), (0, rows - a.shape[1]), (0, 0)))


def _rope_tables(pos, dr, n_heads):
    inv = ROPE_THETA ** (-jnp.arange(0, dr, 2, dtype=F32) / dr)
    ang = pos.astype(F32)[:, None] * inv[None, :]
    return jnp.tile(jnp.cos(ang), (1, n_heads)), jnp.tile(jnp.sin(ang), (1, n_heads))


def kernel(x_prompt, x_sample, state_b_buf, cache_ckv_c, cache_kr_c, cache_k_d, cache_v_d, cache_logf_d, page_table, g_norm, w_up, w_down, w_in_a, g_v_a, b_v_a, w_s_a, b_s_a, w_out_a, w_in_b, w_grp_b, scale_b, w_in_c, g_q_c, g_kv_c, w_uq_c, w_uk_c, w_uv_c, w_o_c, w_in_d, b_f_d, w_o_d):
    B, S, D = x_prompt.shape
    DB, T, _ = x_sample.shape
    depth = g_norm.shape[0]
    n_pages = page_table.shape[1]
    page = cache_ckv_c.shape[2]
    past = n_pages * page
    pos_p = jnp.arange(S, dtype=jnp.int32)
    pos_s = past + jnp.arange(T, dtype=jnp.int32)
    bf = lambda w: w.astype(BF16)

    xp = x_prompt.reshape(B * S, D)
    xs = x_sample.reshape(DB * T, D)
    outs = {k: [] for k in ("v_a_s", "buf_b_p", "buf_b_s", "ckv_c_p", "kr_c_p", "ckv_c_s", "kr_c_s",
                            "k_d_p", "v_d_p", "logf_d_p", "k_d_s", "v_d_s", "logf_d_s")}
    for i in range(depth):
        kind, j = i % 4, i // 4
        g4 = g_norm[i]
        wu, wd = bf(w_up[i]), bf(w_down[i])
        if kind == 0:
            n_groups, chunk = w_s_a.shape[1], w_s_a.shape[2]
            d_a = w_out_a.shape[1]
            c_a = d_a // n_groups
            assert chunk % T == 0
            tril = jnp.tril(jnp.ones((chunk, chunk), bool))
            ws_p = bf(jnp.where(tril[None], w_s_a[j], 0.0))
            bias_p = jnp.repeat(b_s_a[j].T, c_a, axis=1)
            tril_t = jnp.tril(jnp.ones((T, T), bool))
            ws_t = jnp.where(tril_t[None], w_s_a[j][:, :T, :T], 0.0)
            eye = jnp.eye(chunk // T, dtype=F32)
            ws_s = bf(jnp.einsum("ab,gts->gatbs", eye, ws_t).reshape(n_groups, chunk, chunk))
            bias_s = jnp.tile(jnp.repeat(b_s_a[j][:, :T].T, c_a, axis=1), (chunk // T, 1))
            wa = (g4, bf(w_in_a[j]), g_v_a[j][None], b_v_a[j][None])
            xp, _ = _gmlp(xp, *wa, ws_p, bias_p, bf(w_out_a[j]), wu, wd, chunk)
            xs, v_new = _gmlp(xs, *wa, ws_s, bias_s, bf(w_out_a[j]), wu, wd, chunk)
            outs["v_a_s"].append(v_new.reshape(DB, T, d_a))
        elif kind == 1:
            wb = (g4, bf(w_in_b[j]), bf(w_grp_b[j]), scale_b[j][None])
            xp, buf_p = _pool_prompt(xp, *wb, wu, wd, n_seq=B)
            outs["buf_b_p"].append(buf_p[:, HIST_ROWS - POOL_HIST:, :])
            xs_tm = xs.reshape(DB, T, D).transpose(1, 0, 2).reshape(T * DB, D)
            hist_tm = state_b_buf[j].transpose(1, 0, 2).reshape(POOL_HIST * DB, D)
            xs_tm, z_tm = _pool_sample(xs_tm, hist_tm, *wb, db=DB, t_new=T, pos0=past)
            xs = xs_tm.reshape(T, DB, D).transpose(1, 0, 2).reshape(DB * T, D)
            z_s = z_tm.reshape(T, DB, D).transpose(1, 0, 2)
            outs["buf_b_s"].append(jnp.concatenate([state_b_buf[j], z_s], axis=1)[:, -POOL_HIST:])
            xs = _ffn(xs, g4, wu, wd)
        elif kind == 2:
            n_heads, kv_rank, dn = w_uk_c.shape[1:]
            dv = w_uv_c.shape[3]
            dr = cache_kr_c.shape[3]
            q_rank = g_q_c.shape[1]
            scale = float((dn + dr) ** -0.5) * LOG2E
            wuq3 = w_uq_c[j].reshape(q_rank, n_heads, dn + dr)
            wuq_perm = bf(jnp.concatenate([wuq3[:, :, :dn].reshape(q_rank, -1),
                                           wuq3[:, :, dn:dn + dr // 2].reshape(q_rank, -1),
                                           wuq3[:, :, dn + dr // 2:].reshape(q_rank, -1)], axis=1))
            wukt = bf(w_uk_c[j].transpose(0, 2, 1))
            wuvt = bf(w_uv_c[j].transpose(0, 2, 1))
            wuv_all = bf(w_uv_c[j].transpose(1, 0, 2).reshape(kv_rank, n_heads * dv))
            wc = (g4, bf(w_in_c[j]), g_q_c[j][None], g_kv_c[j][None], wuq_perm, wukt)
            cos_p, sin_p = _rope_tables(jnp.tile(pos_p, B), dr, n_heads)
            q_cat, kcat, ckvt, ckv, kr = _mla_proj(xp, *wc, cos_p, sin_p, scale)
            o_p = _mla_flash(q_cat, kcat, ckvt, wuvt, n_seq=B)
            outs["ckv_c_p"].append(ckv.reshape(B, S, kv_rank))
            outs["kr_c_p"].append(kr.reshape(B, S, dr))
            cos_s, sin_s = _rope_tables(jnp.tile(pos_s, DB), dr, n_heads)
            q_cat_s, kcat_s, _, ckv_s, kr_s = _mla_proj(xs, *wc, cos_s, sin_s, scale)
            dk = kv_rank + dr
            q_rows = q_cat_s.reshape(n_heads, DB, T, dk).transpose(1, 2, 0, 3).reshape(DB * T * n_heads, dk)
            knew = _pad_rows(kcat_s.reshape(DB, T, dk), BF16_ROWS)
            o_s = _mla_decode(q_rows, knew, wuv_all, cache_ckv_c, cache_kr_c.transpose(0, 1, 3, 2), j,
                              page_table, T, n_heads)
            outs["ckv_c_s"].append(ckv_s.reshape(DB, T, kv_rank))
            outs["kr_c_s"].append(kr_s.reshape(DB, T, dr))
            xp = _ffn(xp, g4, wu, wd, a=o_p, wo=bf(w_o_c[j]))
            xs = _ffn(xs, g4, wu, wd, a=o_s.reshape(DB * T, n_heads * dv), wo=bf(w_o_c[j]))
        else:
            n_heads, dh = cache_k_d.shape[3], cache_k_d.shape[4]
            hd = n_heads * dh
            scale = float(dh ** -0.5) * LOG2E
            wdd = (g4, bf(w_in_d[j]), b_f_d[j][None], scale)
            q, lf, _, kt, vt, kb, vtb, qa, ka = _fox_proj(xp, *wdd, n_seq=B)
            o_p = _fox_flash(q, qa, kb, ka, vtb, n_seq=B, dh=dh)
            outs["k_d_p"].append(kt.reshape(B, n_heads, dh, S).transpose(0, 3, 1, 2))
            outs["v_d_p"].append(vt.reshape(B, n_heads, dh, S).transpose(0, 3, 1, 2))
            outs["logf_d_p"].append(lf.reshape(B, S, n_heads))
            q_s, lf_s, cs, k_s, v_s, kb_s, vb_s = _fox_proj(xs, *wdd, seg=T)
            cs3 = cs.reshape(DB, T, n_heads)
            cs_t = jnp.pad(cs3.transpose(0, 2, 1), ((0, 0), (0, 0), (0, BF16_ROWS - T)))
            o_s = _fox_decode(q_s.astype(F32).reshape(DB, T, hd), cs3.reshape(DB, T * n_heads, 1), cs_t,
                              _pad_rows(kb_s.reshape(DB, T, hd), BF16_ROWS),
                              _pad_rows(vb_s.reshape(DB, T, hd), BF16_ROWS),
                              cache_k_d.transpose(0, 1, 3, 4, 2), cache_v_d.transpose(0, 1, 3, 4, 2),
                              cache_logf_d.transpose(0, 1, 3, 2), j, page_table)
            outs["k_d_s"].append(k_s.reshape(DB, T, n_heads, dh))
            outs["v_d_s"].append(v_s.reshape(DB, T, n_heads, dh))
            outs["logf_d_s"].append(lf_s.reshape(DB, T, n_heads))
            xp = _ffn(xp, g4, wu, wd, a=o_p, wo=bf(w_o_d[j]))
            xs = _ffn(xs, g4, wu, wd, a=o_s.reshape(DB * T, hd), wo=bf(w_o_d[j]))
    st = lambda name: jnp.stack(outs[name])
    return (xp.reshape(B, S, D), xs.reshape(DB, T, D), st("v_a_s"), st("buf_b_p"), st("buf_b_s"),
            st("ckv_c_p"), st("kr_c_p"), st("ckv_c_s"), st("kr_c_s"),
            st("k_d_p"), st("v_d_p"), st("logf_d_p"), st("k_d_s"), st("v_d_s"), st("logf_d_s"))
```
